```python
import math
import jax, jax.numpy as jnp
from jax import lax
import numpy as np

D_MODEL = 2048
BATCH = 1
SEQ = 8192
DEPTH = 4

N_MIXERS = 3
HEAD_DIM = 128
MOBA_HEADS = D_MODEL // HEAD_DIM
MOBA_BLOCK = 256
MOBA_TOP_K = 3
MOBA_QUERY_CHUNK = 64
CONV_WIDTH = 3
DIL_GROUPS = ((128, 1), (512, 4), (2048, 16))
DIL_HEADS = D_MODEL // HEAD_DIM
ROT_DIM = HEAD_DIM // 4
ROPE_THETA = 500000.0
D_FF = 4 * D_MODEL
RMS_EPS = 1e-5
NEG_INF = -1e30

kernel_name = 'hybrid_moba_shortconv_dilated_trunk'


def rmsnorm(x, g):
    x32 = x.astype(jnp.float32)
    y = x32 * lax.rsqrt(jnp.mean(x32 * x32, axis=-1, keepdims=True) + RMS_EPS)
    return (y * g.astype(jnp.float32)).astype(x.dtype)


def partial_rope(t, positions):
    half = ROT_DIM // 2
    inv_freq = ROPE_THETA ** (-jnp.arange(0, ROT_DIM, 2, dtype=jnp.float32) / ROT_DIM)
    ang = positions.astype(jnp.float32)[:, :, None, None] * inv_freq
    cos, sin = jnp.cos(ang), jnp.sin(ang)
    t32 = t.astype(jnp.float32)
    x1, x2 = t32[..., :half], t32[..., half:ROT_DIM]
    rot = jnp.concatenate([x1 * cos - x2 * sin, x2 * cos + x1 * sin], axis=-1)
    return jnp.concatenate([rot.astype(t.dtype), t[..., ROT_DIM:]], axis=-1)


def gather_blocks(blocks, idx):
    return jax.vmap(jax.vmap(lambda bl, i: bl[i]))(blocks, idx)


def moba_attention(xn, w_qkv, w_o, positions):
    B, S, _ = xn.shape
    H, hd, blk, qc = MOBA_HEADS, HEAD_DIM, MOBA_BLOCK, MOBA_QUERY_CHUNK
    qkv = (xn @ w_qkv).reshape(B, S, 3, H, hd)
    q = partial_rope(qkv[:, :, 0], positions)
    k = partial_rope(qkv[:, :, 1], positions)
    v = qkv[:, :, 2]
    sp = -(-S // blk) * blk
    pad = ((0, 0), (0, sp - S), (0, 0), (0, 0))
    q, k, v = [jnp.pad(t, pad).transpose(0, 2, 1, 3) for t in (q, k, v)]
    nb = sp // blk
    kb = k.reshape(B, H, nb, blk, hd)
    vb = v.reshape(B, H, nb, blk, hd)
    k_mean = jnp.mean(kb.astype(jnp.float32), axis=3)
    gate = jnp.einsum('bhsd,bhnd->bhsn', q.astype(jnp.float32), k_mean)
    q_block = jnp.arange(sp) // blk
    fully_past = jnp.arange(nb)[None, :] < q_block[:, None]
    gate = jnp.where(fully_past, gate, NEG_INF)
    n_sel = min(MOBA_TOP_K, nb)
    _, sel = lax.top_k(gate, n_sel)
    sel_valid = jnp.arange(n_sel)[None, :] < q_block[:, None]
    nc = sp // qc
    q_c = q.reshape(B, H, nc, qc, hd).transpose(2, 0, 1, 3, 4)
    sel_c = sel.reshape(B, H, nc, qc, n_sel).transpose(2, 0, 1, 3, 4)
    valid_c = sel_valid.reshape(nc, qc, n_sel)
    scale = HEAD_DIM ** -0.5

    def chunk(args):
        q_i, sel_i, valid_i, c = args
        b = (c * qc) // blk
        k_own = lax.dynamic_index_in_dim(kb, b, axis=2, keepdims=False)
        v_own = lax.dynamic_index_in_dim(vb, b, axis=2, keepdims=False)
        k_sel = gather_blocks(kb, sel_i)
        v_sel = gather_blocks(vb, sel_i)
        s_sel = jnp.einsum('bhqd,bhqnkd->bhqnk', q_i, k_sel).astype(jnp.float32) * scale
        s_sel = jnp.where(valid_i[None, None, :, :, None], s_sel, NEG_INF)
        s_own = jnp.einsum('bhqd,bhkd->bhqk', q_i, k_own).astype(jnp.float32) * scale
        qpos = c * qc + jnp.arange(qc)
        kpos = b * blk + jnp.arange(blk)
        s_own = jnp.where(kpos[None, :] <= qpos[:, None], s_own, NEG_INF)
        logits = jnp.concatenate([s_sel.reshape(B, H, qc, n_sel * blk), s_own], axis=-1)
        p = jax.nn.softmax(logits, axis=-1).astype(v.dtype)
        p_sel = p[..., :n_sel * blk].reshape(B, H, qc, n_sel, blk)
        p_own = p[..., n_sel * blk:]
        return (jnp.einsum('bhqnk,bhqnkd->bhqd', p_sel, v_sel)
                + jnp.einsum('bhqk,bhkd->bhqd', p_own, v_own))

    o = lax.map(chunk, (q_c, sel_c, valid_c, jnp.arange(nc)))
    o = o.transpose(1, 0, 3, 2, 4).reshape(B, sp, H * hd)[:, :S]
    return o @ w_o


def short_conv_mixer(xn, w_in, conv_w, w_out):
    D = xn.shape[-1]
    b_gate, c_gate, u = jnp.split(xn @ w_in, 3, axis=-1)
    z = c_gate * u
    conv = lax.conv_general_dilated(
        z, conv_w[:, None, :].astype(z.dtype), window_strides=(1,),
        padding=((CONV_WIDTH - 1, 0),), dimension_numbers=('NWC', 'WIO', 'NWC'),
        feature_group_count=D)
    return (b_gate * conv) @ w_out


def dilated_group(q, k, v, window, dilation):
    B, H, S, hd = q.shape
    band = window // dilation
    L = S // dilation
    lp = -(-L // band) * band
    nbl = lp // band
    scale = HEAD_DIM ** -0.5

    def to_blocks(t):
        t = t.reshape(B, H, L, dilation, hd).transpose(0, 1, 3, 2, 4)
        t = jnp.pad(t, ((0, 0), (0, 0), (0, 0), (0, lp - L), (0, 0)))
        return t.reshape(B, H, dilation, nbl, band, hd)

    def with_prev(t):
        prev = jnp.concatenate([jnp.zeros_like(t[:, :, :, :1]), t[:, :, :, :-1]], axis=3)
        return jnp.concatenate([prev, t], axis=4)

    qs = to_blocks(q)
    kband = with_prev(to_blocks(k))
    vband = with_prev(to_blocks(v))
    s = jnp.einsum('bhrnqd,bhrnkd->bhrnqk', qs, kband).astype(jnp.float32) * scale
    j = jnp.arange(nbl)[:, None, None]
    qi = j * band + jnp.arange(band)[None, :, None]
    ki = (j - 1) * band + jnp.arange(2 * band)[None, None, :]
    diff = qi - ki
    valid = (diff >= 0) & (diff <= band) & (ki >= 0)
    s = jnp.where(valid, s, NEG_INF)
    m = jnp.max(s, axis=-1, keepdims=True)
    e = jnp.exp(s - m)
    l = jnp.sum(e, axis=-1, keepdims=True)
    o = jnp.einsum('bhrnqk,bhrnkd->bhrnqd', e, vband.astype(jnp.float32)) / l
    lse = (m + jnp.log(l))[..., 0]

    def from_blocks(t):
        t = t.reshape(B, H, dilation, lp, *t.shape[5:])[:, :, :, :L]
        t = jnp.moveaxis(t, 2, 3)
        return t.reshape(B, H, S, *t.shape[4:])

    return from_blocks(o), from_blocks(lse)


def dilated_mixer(xn, w_qkv, w_o, positions):
    B, S, _ = xn.shape
    G = len(DIL_GROUPS)
    qkv = (xn @ w_qkv).reshape(B, S, G, 3, DIL_HEADS, HEAD_DIM)
    outs, lses = [], []
    for g, (window, dilation) in enumerate(DIL_GROUPS):
        q = partial_rope(qkv[:, :, g, 0], positions).transpose(0, 2, 1, 3)
        k = partial_rope(qkv[:, :, g, 1], positions).transpose(0, 2, 1, 3)
        v = qkv[:, :, g, 2].transpose(0, 2, 1, 3)
        o, lse = dilated_group(q, k, v, window, dilation)
        outs.append(o)
        lses.append(lse)
    alpha = jax.nn.softmax(jnp.stack(lses), axis=0)
    o = jnp.einsum('gbhs,gbhsd->bhsd', alpha, jnp.stack(outs)).astype(xn.dtype)
    o = o.transpose(0, 2, 1, 3).reshape(B, S, DIL_HEADS * HEAD_DIM)
    return o @ w_o


def sq_relu_mlp(xn, w_up, w_down):
    h = jax.nn.relu(xn @ w_up)
    return (h * h) @ w_down


def setup_inputs(seed: int = 0) -> dict:
    key = jax.random.key(seed)
    ks = jax.random.split(key, 16)
    n_a = len(range(0, DEPTH, N_MIXERS))
    n_b = len(range(1, DEPTH, N_MIXERS))
    n_c = len(range(2, DEPTH, N_MIXERS))

    def w(k, shape, fan_in):
        return jax.random.normal(k, shape, jnp.float32) * (fan_in ** -0.5)

    def gain(k, shape):
        return 1.0 + 0.01 * jax.random.normal(k, shape, jnp.float32)

    moba_width = MOBA_HEADS * HEAD_DIM
    dil_width = DIL_HEADS * HEAD_DIM
    G = len(DIL_GROUPS)
    return {
        'x': jax.random.normal(ks[0], (BATCH, SEQ, D_MODEL), jnp.float32),
        'positions': jnp.broadcast_to(jnp.arange(SEQ, dtype=jnp.int32), (BATCH, SEQ)),
        'norm_mix': gain(ks[1], (DEPTH, D_MODEL)),
        'norm_mlp': gain(ks[2], (DEPTH, D_MODEL)),
        'norm_final': gain(ks[3], (D_MODEL,)),
        'mlp_w_up': w(ks[4], (DEPTH, D_MODEL, D_FF), D_MODEL),
        'mlp_w_down': w(ks[5], (DEPTH, D_FF, D_MODEL), D_FF),
        'moba_w_qkv': w(ks[6], (n_a, D_MODEL, 3 * moba_width), D_MODEL),
        'moba_w_o': w(ks[7], (n_a, moba_width, D_MODEL), moba_width),
        'conv_w_in': w(ks[8], (n_b, D_MODEL, 3 * D_MODEL), D_MODEL),
        'conv_w': w(ks[9], (n_b, CONV_WIDTH, D_MODEL), CONV_WIDTH),
        'conv_w_out': w(ks[10], (n_b, D_MODEL, D_MODEL), D_MODEL),
        'dil_w_qkv': w(ks[11], (n_c, D_MODEL, G * 3 * dil_width), D_MODEL),
        'dil_w_o': w(ks[12], (n_c, dil_width, D_MODEL), dil_width),
    }


def reference(x, positions, norm_mix, norm_mlp, norm_final, mlp_w_up, mlp_w_down,
              moba_w_qkv, moba_w_o, conv_w_in, conv_w, conv_w_out, dil_w_qkv, dil_w_o):
    h = x
    for i in range(DEPTH):
        kind, j = i % N_MIXERS, i // N_MIXERS
        hn = rmsnorm(h, norm_mix[i])
        if kind == 0:
            mix = moba_attention(hn, moba_w_qkv[j], moba_w_o[j], positions)
        elif kind == 1:
            mix = short_conv_mixer(hn, conv_w_in[j], conv_w[j], conv_w_out[j])
        else:
            mix = dilated_mixer(hn, dil_w_qkv[j], dil_w_o[j], positions)
        h = h + mix
        h = h + sq_relu_mlp(rmsnorm(h, norm_mlp[i]), mlp_w_up[i], mlp_w_down[i])
    return rmsnorm(h, norm_final)
```

```python
import functools

import jax
import jax.numpy as jnp
from jax import lax
from jax.experimental import pallas as pl
from jax.experimental.pallas import tpu as pltpu

HEAD_DIM = 128
MOBA_BLOCK = 256
MOBA_TOP_K = 3
CONV_WIDTH = 3
DIL_GROUPS = ((128, 1), (512, 4), (2048, 16))
ROT_DIM = HEAD_DIM // 4
ROPE_THETA = 500000.0
RMS_EPS = 1e-5
NEG_INF = -1e30
N_MIXERS = 3

LANES = 128
VMEM_LIMIT = 56 * 1024 * 1024

F32 = jnp.float32
BF16 = jnp.bfloat16


def _params(*sem):
    return pltpu.CompilerParams(dimension_semantics=sem, vmem_limit_bytes=VMEM_LIMIT)


def _rmsnorm_rows(x, g):
    y = x * lax.rsqrt(jnp.mean(x * x, axis=-1, keepdims=True) + RMS_EPS)
    return y * g


def _dot(a, b):
    return jnp.dot(a, b, preferred_element_type=F32)


def _dot_nt(a, b):
    return lax.dot_general(a, b, (((1,), (1,)), ((), ())), preferred_element_type=F32)


def _rope_table_kernel(pos_ref, invf_ref, cos_ref, sa_ref, sb_ref):
    ang = pos_ref[...] * invf_ref[...]
    lane = lax.broadcasted_iota(jnp.int32, ang.shape, 1)
    c, s = jnp.cos(ang), jnp.sin(ang)
    half = ROT_DIM // 2
    cos_ref[...] = jnp.where(lane < ROT_DIM, c, 1.0)
    sa_ref[...] = jnp.where(lane < half, -s, 0.0)
    sb_ref[...] = jnp.where((lane >= half) & (lane < ROT_DIM), s, 0.0)


def rope_tables(positions):
    S = positions.shape[0]
    tm = min(S, 1024)
    half = ROT_DIM // 2
    inv_freq = ROPE_THETA ** (-jnp.arange(0, ROT_DIM, 2, dtype=F32) / ROT_DIM)
    invf = jnp.concatenate([inv_freq, inv_freq, jnp.zeros((LANES - 2 * half,), F32)])[None, :]
    pos = positions.astype(F32)[:, None]
    tab = jax.ShapeDtypeStruct((S, LANES), F32)
    return pl.pallas_call(
        _rope_table_kernel,
        grid=(S // tm,),
        in_specs=[pl.BlockSpec((tm, 1), lambda i: (i, 0)),
                  pl.BlockSpec((1, LANES), lambda i: (0, 0))],
        out_specs=[pl.BlockSpec((tm, LANES), lambda i: (i, 0))] * 3,
        out_shape=[tab, tab, tab],
        compiler_params=_params("arbitrary"),
        name="rope_tables",
    )(pos, invf)


def _apply_rope(t, cos, sa, sb):
    half = ROT_DIM // 2
    return (t * cos + pltpu.roll(t, LANES - half, 1) * sa + pltpu.roll(t, half, 1) * sb)


def _norm_mm_kernel(x_ref, g_ref, w_ref, o_ref, xn_ref):
    @pl.when(pl.program_id(1) == 0)
    def _():
        xn_ref[...] = _rmsnorm_rows(x_ref[...], g_ref[...]).astype(BF16)

    o_ref[...] = _dot(xn_ref[...], w_ref[...]).astype(o_ref.dtype)


def _norm_mm_rope_kernel(x_ref, g_ref, w_ref, cos_ref, sa_ref, sb_ref, o_ref, xn_ref, *, tn, d_model):
    j = pl.program_id(1)

    @pl.when(j == 0)
    def _():
        xn_ref[...] = _rmsnorm_rows(x_ref[...], g_ref[...]).astype(BF16)

    acc = _dot(xn_ref[...], w_ref[...])
    roped = ((j * tn) // d_model) % 3 != 2
    cos = jnp.where(roped, cos_ref[...], 1.0)
    sa = jnp.where(roped, sa_ref[...], 0.0)
    sb = jnp.where(roped, sb_ref[...], 0.0)
    for c in range(tn // HEAD_DIM):
        sl = slice(c * HEAD_DIM, (c + 1) * HEAD_DIM)
        o_ref[:, sl] = _apply_rope(acc[:, sl], cos, sa, sb).astype(o_ref.dtype)


def norm_matmul(x, g, w, tables=None, *, tm=512, tn=1024, out_dtype=BF16):
    S, D = x.shape
    N = w.shape[1]
    tm, tn = min(tm, S), min(tn, N)
    assert S % tm == 0 and N % tn == 0 and D % tn == 0
    in_specs = [pl.BlockSpec((tm, D), lambda i, j: (i, 0)),
                pl.BlockSpec((1, D), lambda i, j: (0, 0)),
                pl.BlockSpec((D, tn), lambda i, j: (0, j))]
    args = [x, g.reshape(1, D), w]
    if tables is None:
        body = _norm_mm_kernel
    else:
        body = functools.partial(_norm_mm_rope_kernel, tn=tn, d_model=D)
        in_specs += [pl.BlockSpec((tm, LANES), lambda i, j: (i, 0))] * 3
        args += list(tables)
    return pl.pallas_call(
        body,
        grid=(S // tm, N // tn),
        in_specs=in_specs,
        out_specs=pl.BlockSpec((tm, tn), lambda i, j: (i, j)),
        out_shape=jax.ShapeDtypeStruct((S, N), out_dtype),
        scratch_shapes=[pltpu.VMEM((tm, D), BF16)],
        compiler_params=_params("parallel", "arbitrary"),
        name="norm_matmul",
    )(*args)


def _mm_res_kernel(a_ref, w_ref, h_ref, o_ref):
    o_ref[...] = h_ref[...] + _dot(a_ref[...], w_ref[...])


def matmul_residual(a, w, h, *, tm=1024, tn=1024):
    S, K = a.shape
    N = w.shape[1]
    tm, tn = min(tm, S), min(tn, N)
    assert S % tm == 0 and N % tn == 0
    return pl.pallas_call(
        _mm_res_kernel,
        grid=(S // tm, N // tn),
        in_specs=[pl.BlockSpec((tm, K), lambda i, j: (i, 0)),
                  pl.BlockSpec((K, tn), lambda i, j: (0, j)),
                  pl.BlockSpec((tm, tn), lambda i, j: (i, j))],
        out_specs=pl.BlockSpec((tm, tn), lambda i, j: (i, j)),
        out_shape=jax.ShapeDtypeStruct((S, N), F32),
        compiler_params=_params("parallel", "arbitrary"),
        name="matmul_residual",
    )(a, w, h)


def _mlp_kernel(h_ref, g_ref, wu_ref, wd_ref, gf_ref, o_ref, xn_ref, *, final_norm):
    f = pl.program_id(1)

    @pl.when(f == 0)
    def _():
        h = h_ref[...]
        xn_ref[...] = _rmsnorm_rows(h, g_ref[...]).astype(BF16)
        o_ref[...] = h

    hid = jnp.maximum(_dot(xn_ref[...], wu_ref[...]), 0.0)
    o_ref[...] += _dot((hid * hid).astype(BF16), wd_ref[...])

    if final_norm:
        @pl.when(f == pl.num_programs(1) - 1)
        def _():
            o_ref[...] = _rmsnorm_rows(o_ref[...], gf_ref[...])


def mlp_residual(h, g, w_up, w_down, g_final=None, *, tm=512, tf=512):
    S, D = h.shape
    Fd = w_up.shape[1]
    tm, tf = min(tm, S), min(tf, Fd)
    assert S % tm == 0 and Fd % tf == 0
    final_norm = g_final is not None
    gf = (g_final if final_norm else g).reshape(1, D)
    return pl.pallas_call(
        functools.partial(_mlp_kernel, final_norm=final_norm),
        grid=(S // tm, Fd // tf),
        in_specs=[pl.BlockSpec((tm, D), lambda i, f: (i, 0)),
                  pl.BlockSpec((1, D), lambda i, f: (0, 0)),
                  pl.BlockSpec((D, tf), lambda i, f: (0, f)),
                  pl.BlockSpec((tf, D), lambda i, f: (f, 0)),
                  pl.BlockSpec((1, D), lambda i, f: (0, 0))],
        out_specs=pl.BlockSpec((tm, D), lambda i, f: (i, 0)),
        out_shape=jax.ShapeDtypeStruct((S, D), F32),
        scratch_shapes=[pltpu.VMEM((tm, D), BF16)],
        compiler_params=_params("parallel", "arbitrary"),
        name="mlp_residual",
    )(h, g.reshape(1, D), w_up, w_down, gf)


def _moba_kernel(q_ref, k_ref, v_ref, o_ref, kmean_ref, m_ref, l_ref, acc_ref, *, nb):
    blk = MOBA_BLOCK
    qb = pl.program_id(1)
    scale = HEAD_DIM ** -0.5

    @pl.when(qb == 0)
    def _():
        for n in range(nb):
            kn = k_ref[n * blk:(n + 1) * blk, :].astype(F32)
            kmean_ref[n:n + 1, :] = jnp.mean(kn, axis=0, keepdims=True)

    q = q_ref[...]

    km = kmean_ref[...]
    km_hi = km.astype(BF16)
    r1 = km - km_hi.astype(F32)
    km_mid = r1.astype(BF16)
    km_lo = (r1 - km_mid.astype(F32)).astype(BF16)
    gate = _dot_nt(q, km_hi) + _dot_nt(q, km_mid) + _dot_nt(q, km_lo)

    blk_id = lax.broadcasted_iota(jnp.int32, gate.shape, 1)
    gate = jnp.where(blk_id < qb, gate, NEG_INF)
    picks = []
    for n in range(min(MOBA_TOP_K, nb)):
        best = jnp.max(gate, axis=1, keepdims=True)
        idx = jnp.min(jnp.where(gate == best, blk_id, nb), axis=1, keepdims=True)
        picks.append(jnp.where(n < qb, idx, -1))
        gate = jnp.where(blk_id == idx, -jnp.inf, gate)

    row = lax.broadcasted_iota(jnp.int32, (blk, blk), 0)
    col = lax.broadcasted_iota(jnp.int32, (blk, blk), 1)
    start = pl.multiple_of(qb * blk, blk)
    s = _dot_nt(q, k_ref[pl.ds(start, blk), :]) * scale
    s = jnp.where(col <= row, s, NEG_INF)
    m0 = jnp.max(s, axis=1, keepdims=True)
    p = jnp.exp(s - m0)
    m_ref[...] = m0
    l_ref[...] = jnp.sum(p, axis=1, keepdims=True)
    acc_ref[...] = _dot(p.astype(BF16), v_ref[pl.ds(start, blk), :])

    def body(j, carry):
        st = pl.multiple_of(j * blk, blk)
        sj = _dot_nt(q, k_ref[pl.ds(st, blk), :]) * scale
        chosen = picks[0] == j
        for idx in picks[1:]:
            chosen = chosen | (idx == j)
        sj = jnp.where(chosen, sj, NEG_INF)
        m_old = m_ref[...]
        m_new = jnp.maximum(m_old, jnp.max(sj, axis=1, keepdims=True))
        alpha = jnp.exp(m_old - m_new)
        pj = jnp.exp(sj - m_new)
        l_ref[...] = alpha * l_ref[...] + jnp.sum(pj, axis=1, keepdims=True)
        acc_ref[...] = alpha * acc_ref[...] + _dot(pj.astype(BF16), v_ref[pl.ds(st, blk), :])
        m_ref[...] = m_new
        return carry

    lax.fori_loop(0, qb, body, 0)
    o_ref[...] = (acc_ref[...] * (1.0 / l_ref[...])).astype(o_ref.dtype)


def moba_attention(qkv, n_heads):
    S = qkv.shape[0]
    blk, hd, H = MOBA_BLOCK, HEAD_DIM, n_heads
    assert S % blk == 0
    nb = S // blk
    return pl.pallas_call(
        functools.partial(_moba_kernel, nb=nb),
        grid=(H, nb),
        in_specs=[pl.BlockSpec((blk, hd), lambda h, i: (i, h)),
                  pl.BlockSpec((S, hd), lambda h, i: (0, H + h)),
                  pl.BlockSpec((S, hd), lambda h, i: (0, 2 * H + h))],
        out_specs=pl.BlockSpec((blk, hd), lambda h, i: (i, h)),
        out_shape=jax.ShapeDtypeStruct((S, H * hd), BF16),
        scratch_shapes=[pltpu.VMEM((nb, hd), F32),
                        pltpu.VMEM((blk, 1), F32),
                        pltpu.VMEM((blk, 1), F32),
                        pltpu.VMEM((blk, hd), F32)],
        compiler_params=_params("parallel", "arbitrary"),
        name="moba_attention",
    )(qkv, qkv, qkv)


def _conv_in_kernel(x_ref, g_ref, wb_ref, wc_ref, wu_ref, b_ref, z_ref, xn_ref):
    @pl.when(pl.program_id(1) == 0)
    def _():
        xn_ref[...] = _rmsnorm_rows(x_ref[...], g_ref[...]).astype(BF16)

    xn = xn_ref[...]
    b_ref[...] = _dot(xn, wb_ref[...])
    z_ref[...] = _dot(xn, wc_ref[...]) * _dot(xn, wu_ref[...])


def conv_in(x, g, w_in, *, tm=512, tn=512):
    S, D = x.shape
    tm, tn = min(tm, S), min(tn, D)
    assert S % tm == 0 and D % tn == 0 and w_in.shape[1] == 3 * D
    nj = D // tn
    out = jax.ShapeDtypeStruct((S, D), F32)
    return pl.pallas_call(
        _conv_in_kernel,
        grid=(S // tm, nj),
        in_specs=[pl.BlockSpec((tm, D), lambda i, j: (i, 0)),
                  pl.BlockSpec((1, D), lambda i, j: (0, 0)),
                  pl.BlockSpec((D, tn), lambda i, j: (0, j)),
                  pl.BlockSpec((D, tn), lambda i, j: (0, nj + j)),
                  pl.BlockSpec((D, tn), lambda i, j: (0, 2 * nj + j))],
        out_specs=[pl.BlockSpec((tm, tn), lambda i, j: (i, j))] * 2,
        out_shape=[out, out],
        scratch_shapes=[pltpu.VMEM((tm, D), BF16)],
        compiler_params=_params("parallel", "arbitrary"),
        name="conv_in",
    )(x, g.reshape(1, D), w_in, w_in, w_in)


HALO = 8


def _conv_out_kernel(b_ref, z_ref, zh_ref, cw_ref, w_ref, h_ref, o_ref, a_ref):
    @pl.when(pl.program_id(1) == 0)
    def _():
        z = z_ref[...]
        tm = z.shape[0]
        first = pl.program_id(0) == 0
        halo = jnp.where(first, 0.0, zh_ref[...])
        row = lax.broadcasted_iota(jnp.int32, z.shape, 0)
        zm1 = jnp.where(row == 0, halo[HALO - 1:HALO, :], pltpu.roll(z, 1, 0))
        zm2 = jnp.where(row == 0, halo[HALO - 2:HALO - 1, :],
                        jnp.where(row == 1, halo[HALO - 1:HALO, :], pltpu.roll(z, 2, 0)))
        cw = cw_ref[...]
        conv = cw[0:1, :] * zm2 + cw[1:2, :] * zm1 + cw[2:3, :] * z
        a_ref[...] = (b_ref[...] * conv).astype(BF16)

    o_ref[...] = h_ref[...] + _dot(a_ref[...], w_ref[...])


def conv_out(b, z, conv_w, w_out, h, *, tm=512, tn=1024):
    S, D = z.shape
    N = w_out.shape[1]
    tm, tn = min(tm, S), min(tn, N)
    assert S % tm == 0 and N % tn == 0 and tm % HALO == 0 and conv_w.shape[0] == CONV_WIDTH
    return pl.pallas_call(
        _conv_out_kernel,
        grid=(S // tm, N // tn),
        in_specs=[pl.BlockSpec((tm, D), lambda i, j: (i, 0)),
                  pl.BlockSpec((tm, D), lambda i, j: (i, 0)),
                  pl.BlockSpec((HALO, D), lambda i, j: (jnp.maximum(i * (tm // HALO) - 1, 0), 0)),
                  pl.BlockSpec((CONV_WIDTH, D), lambda i, j: (0, 0)),
                  pl.BlockSpec((D, tn), lambda i, j: (0, j)),
                  pl.BlockSpec((tm, tn), lambda i, j: (i, j))],
        out_specs=pl.BlockSpec((tm, tn), lambda i, j: (i, j)),
        out_shape=jax.ShapeDtypeStruct((S, N), F32),
        scratch_shapes=[pltpu.VMEM((tm, D), BF16)],
        compiler_params=_params("parallel", "arbitrary"),
        name="conv_out",
    )(b, z, z, conv_w, w_out, h)


def _dilated_kernel(q_ref, kp_ref, ko_ref, vp_ref, vo_ref, o_ref, lse_ref, *, band, n_heads):
    j = pl.program_id(1)
    scale = HEAD_DIM ** -0.5
    a = lax.broadcasted_iota(jnp.int32, (band, 2 * band), 0)
    b = lax.broadcasted_iota(jnp.int32, (band, 2 * band), 1)
    valid = (b >= a) & (b <= a + band) & ((j > 0) | (b >= band))
    for hh in range(n_heads):
        sl = slice(hh * HEAD_DIM, (hh + 1) * HEAD_DIM)
        kband = jnp.concatenate([kp_ref[:, sl], ko_ref[:, sl]], axis=0)
        vband = jnp.concatenate([vp_ref[:, sl], vo_ref[:, sl]], axis=0)
        s = _dot_nt(q_ref[:, sl], kband) * scale
        s = jnp.where(valid, s, NEG_INF)
        m = jnp.max(s, axis=1, keepdims=True)
        e = jnp.exp(s - m)
        l = jnp.sum(e, axis=1, keepdims=True)
        o_ref[:, sl] = _dot(e.astype(BF16), vband) * (1.0 / l)
        lse_ref[:, sl] = jnp.broadcast_to(m + jnp.log(l), (band, HEAD_DIM))


def dilated_group_attention(qkv, g, n_groups, window, dilation, n_heads):
    S, W = qkv.shape
    band = window // dilation
    L = S // dilation
    width = n_heads * HEAD_DIM
    assert S % dilation == 0 and L % band == 0 and W == n_groups * 3 * width
    nbl = L // band
    sec = W // width
    view = qkv.reshape(L, dilation * W)

    def spec(t, prev):
        if prev:
            return pl.BlockSpec((band, width), lambda r, j: (jnp.maximum(j - 1, 0), r * sec + g * 3 + t))
        return pl.BlockSpec((band, width), lambda r, j: (j, r * sec + g * 3 + t))

    out = jax.ShapeDtypeStruct((L, dilation * width), F32)
    o, lse = pl.pallas_call(
        functools.partial(_dilated_kernel, band=band, n_heads=n_heads),
        grid=(dilation, nbl),
        in_specs=[spec(0, False), spec(1, True), spec(1, False), spec(2, True), spec(2, False)],
        out_specs=[pl.BlockSpec((band, width), lambda r, j: (j, r))] * 2,
        out_shape=[out, out],
        compiler_params=_params("parallel", "arbitrary"),
        name="dilated_attention",
    )(view, view, view, view, view)
    return o.reshape(S, width), lse.reshape(S, width)


def _combine_kernel(*refs):
    n = (len(refs) - 1) // 2
    o_refs, l_refs, out_ref = refs[:n], refs[n:2 * n], refs[2 * n]
    lses = [r[...] for r in l_refs]
    top = lses[0]
    for l in lses[1:]:
        top = jnp.maximum(top, l)
    ws = [jnp.exp(l - top) for l in lses]
    den = ws[0]
    for w in ws[1:]:
        den = den + w
    inv = 1.0 / den
    acc = (ws[0] * inv) * o_refs[0][...]
    for w, o in zip(ws[1:], o_refs[1:]):
        acc = acc + (w * inv) * o[...]
    out_ref[...] = acc.astype(out_ref.dtype)


def combine_groups(outs, lses, *, tm=256):
    S, width = outs[0].shape
    tm = min(tm, S)
    assert S % tm == 0
    spec = pl.BlockSpec((tm, width), lambda i: (i, 0))
    return pl.pallas_call(
        _combine_kernel,
        grid=(S // tm,),
        in_specs=[spec] * (2 * len(outs)),
        out_specs=spec,
        out_shape=jax.ShapeDtypeStruct((S, width), BF16),
        compiler_params=_params("parallel"),
        name="combine_groups",
    )(*outs, *lses)


def kernel(x, positions, norm_mix, norm_mlp, norm_final, mlp_w_up, mlp_w_down, moba_w_qkv, moba_w_o,
           conv_w_in, conv_w, conv_w_out, dil_w_qkv, dil_w_o):
    B, S, D = x.shape
    assert B == 1 and D % HEAD_DIM == 0
    n_heads = D // HEAD_DIM
    depth = norm_mix.shape[0]
    h = x[0]
    tables = rope_tables(positions[0])
    for i in range(depth):
        kind, j = i % N_MIXERS, i // N_MIXERS
        if kind == 0:
            qkv = norm_matmul(h, norm_mix[i], moba_w_qkv[j].astype(BF16), tables)
            o = moba_attention(qkv, n_heads)
            h = matmul_residual(o, moba_w_o[j].astype(BF16), h)
        elif kind == 1:
            b, z = conv_in(h, norm_mix[i], conv_w_in[j].astype(BF16))
            h = conv_out(b, z, conv_w[j], conv_w_out[j].astype(BF16), h)
        else:
            qkv = norm_matmul(h, norm_mix[i], dil_w_qkv[j].astype(BF16), tables)
            outs, lses = [], []
            for g, (window, dilation) in enumerate(DIL_GROUPS):
                o_g, lse_g = dilated_group_attention(qkv, g, len(DIL_GROUPS), window, dilation, n_heads)
                outs.append(o_g)
                lses.append(lse_g)
            o = combine_groups(outs, lses)
            h = matmul_residual(o, dil_w_o[j].astype(BF16), h)
        g_final = norm_final if i == depth - 1 else None
        h = mlp_residual(h, norm_mlp[i], mlp_w_up[i].astype(BF16), mlp_w_down[i].astype(BF16), g_final)
    return h[None]
```

```python
import functools

import jax
import jax.numpy as jnp
from jax import lax
from jax.experimental import pallas as pl
from jax.experimental.pallas import tpu as pltpu

HEAD_DIM = 128
MOBA_BLOCK = 256
MOBA_TOP_K = 3
MOBA_CHUNK = 4
CONV_WIDTH = 3
DIL_GROUPS = ((128, 1), (512, 4), (2048, 16))
ROT_DIM = HEAD_DIM // 4
ROPE_THETA = 500000.0
RMS_EPS = 1e-5
NEG_INF = -1e30
N_MIXERS = 3

LANES = 128
VMEM_LIMIT = 56 * 1024 * 1024

F32 = jnp.float32
BF16 = jnp.bfloat16


def _params(*sem):
    return pltpu.CompilerParams(dimension_semantics=sem, vmem_limit_bytes=VMEM_LIMIT)


def _rmsnorm_rows(x, g):
    y = x * lax.rsqrt(jnp.mean(x * x, axis=-1, keepdims=True) + RMS_EPS)
    return y * g


def _dot(a, b):
    return jnp.dot(a, b, preferred_element_type=F32)


def _dot_nt(a, b):
    return lax.dot_general(a, b, (((1,), (1,)), ((), ())), preferred_element_type=F32)


def _lane_groups(x):
    return [x[:, g * LANES:(g + 1) * LANES] for g in range(x.shape[1] // LANES)]


def _rope_table_kernel(pos_ref, invf_ref, cos_ref, sa_ref, sb_ref):
    ang = pos_ref[...] * invf_ref[...]
    lane = lax.broadcasted_iota(jnp.int32, ang.shape, 1)
    c, s = jnp.cos(ang), jnp.sin(ang)
    half = ROT_DIM // 2
    cos_ref[...] = jnp.where(lane < ROT_DIM, c, 1.0)
    sa_ref[...] = jnp.where(lane < half, -s, 0.0)
    sb_ref[...] = jnp.where((lane >= half) & (lane < ROT_DIM), s, 0.0)


def rope_tables(positions):
    S = positions.shape[0]
    tm = min(S, 1024)
    half = ROT_DIM // 2
    inv_freq = ROPE_THETA ** (-jnp.arange(0, ROT_DIM, 2, dtype=F32) / ROT_DIM)
    invf = jnp.concatenate([inv_freq, inv_freq, jnp.zeros((LANES - 2 * half,), F32)])[None, :]
    pos = positions.astype(F32)[:, None]
    tab = jax.ShapeDtypeStruct((S, LANES), F32)
    return pl.pallas_call(
        _rope_table_kernel,
        grid=(S // tm,),
        in_specs=[pl.BlockSpec((tm, 1), lambda i: (i, 0)),
                  pl.BlockSpec((1, LANES), lambda i: (0, 0))],
        out_specs=[pl.BlockSpec((tm, LANES), lambda i: (i, 0))] * 3,
        out_shape=[tab, tab, tab],
        compiler_params=_params("arbitrary"),
        name="rope_tables",
    )(pos, invf)


def _apply_rope(t, cos, sa, sb):
    half = ROT_DIM // 2
    return (t * cos + pltpu.roll(t, LANES - half, 1) * sa + pltpu.roll(t, half, 1) * sb)


def _norm_mm_kernel(x_ref, g_ref, w_ref, o_ref, xn_ref):
    @pl.when(pl.program_id(1) == 0)
    def _():
        xn_ref[...] = _rmsnorm_rows(x_ref[...], g_ref[...]).astype(BF16)

    o_ref[...] = _dot(xn_ref[...], w_ref[...]).astype(o_ref.dtype)


def _norm_mm_rope_kernel(x_ref, g_ref, w_ref, cos_ref, sa_ref, sb_ref, o_ref, xn_ref, *scratch,
                         tn, d_model, dilation):
    j = pl.program_id(1)

    @pl.when(j == 0)
    def _():
        xn_ref[...] = _rmsnorm_rows(x_ref[...], g_ref[...]).astype(BF16)

    acc = _dot(xn_ref[...], w_ref[...])
    roped = ((j * tn) // d_model) % 3 != 2
    heads = tn // HEAD_DIM
    if dilation == 1:
        cos = jnp.where(roped, cos_ref[...], 1.0)
        sa = jnp.where(roped, sa_ref[...], 0.0)
        sb = jnp.where(roped, sb_ref[...], 0.0)
        for c in range(heads):
            sl = slice(c * HEAD_DIM, (c + 1) * HEAD_DIM)
            o_ref[0, :, sl] = _apply_rope(acc[:, sl], cos, sa, sb).astype(o_ref.dtype)
    else:
        acc_ref, = scratch
        for c in range(heads):
            acc_ref[c] = acc[:, c * HEAD_DIM:(c + 1) * HEAD_DIM]
        rows = acc.shape[0] // dilation
        for r in range(dilation):
            pick = pl.ds(r, rows, stride=dilation)
            cos = jnp.where(roped, cos_ref[pick, :], 1.0)
            sa = jnp.where(roped, sa_ref[pick, :], 0.0)
            sb = jnp.where(roped, sb_ref[pick, :], 0.0)
            for c in range(heads):
                sl = slice(c * HEAD_DIM, (c + 1) * HEAD_DIM)
                o_ref[r, :, sl] = _apply_rope(acc_ref[c, pick, :], cos, sa, sb).astype(o_ref.dtype)


def norm_matmul(x, g, w, *, tm=512, tn=1024, out_dtype=BF16):
    S, D = x.shape
    N = w.shape[1]
    tm, tn = min(tm, S), min(tn, N)
    assert S % tm == 0 and N % tn == 0
    return pl.pallas_call(
        _norm_mm_kernel,
        grid=(S // tm, N // tn),
        in_specs=[pl.BlockSpec((tm, D), lambda i, j: (i, 0)),
                  pl.BlockSpec((1, D), lambda i, j: (0, 0)),
                  pl.BlockSpec((D, tn), lambda i, j: (0, j))],
        out_specs=pl.BlockSpec((tm, tn), lambda i, j: (i, j)),
        out_shape=jax.ShapeDtypeStruct((S, N), out_dtype),
        scratch_shapes=[pltpu.VMEM((tm, D), BF16)],
        compiler_params=_params("parallel", "arbitrary"),
        name="norm_matmul",
    )(x, g.reshape(1, D), w)


def norm_matmul_rope(x, g, w, tables, *, dilation=1, tm=512, tn=1024):
    S, D = x.shape
    N = w.shape[1]
    tm, tn = min(tm, S), min(tn, N)
    assert S % tm == 0 and N % tn == 0 and D % tn == 0 and tm % (16 * dilation) == 0
    rows = tm // dilation
    scratch = [pltpu.VMEM((tm, D), BF16)]
    if dilation > 1:
        scratch.append(pltpu.VMEM((tn // HEAD_DIM, tm, HEAD_DIM), F32))
    return pl.pallas_call(
        functools.partial(_norm_mm_rope_kernel, tn=tn, d_model=D, dilation=dilation),
        grid=(S // tm, N // tn),
        in_specs=[pl.BlockSpec((tm, D), lambda i, j: (i, 0)),
                  pl.BlockSpec((1, D), lambda i, j: (0, 0)),
                  pl.BlockSpec((D, tn), lambda i, j: (0, j))]
                 + [pl.BlockSpec((tm, LANES), lambda i, j: (i, 0))] * 3,
        out_specs=pl.BlockSpec((dilation, rows, tn), lambda i, j: (0, i, j)),
        out_shape=jax.ShapeDtypeStruct((dilation, S // dilation, N), BF16),
        scratch_shapes=scratch,
        compiler_params=_params("parallel", "arbitrary"),
        name="norm_matmul_rope",
    )(x, g.reshape(1, D), w, *tables)


def _mm_res_kernel(a_ref, w_ref, h_ref, o_ref):
    o_ref[...] = h_ref[...] + _dot(a_ref[...], w_ref[...])


def matmul_residual(a, w, h, *, tm=1024, tn=1024):
    S, K = a.shape
    N = w.shape[1]
    tm, tn = min(tm, S), min(tn, N)
    assert S % tm == 0 and N % tn == 0
    return pl.pallas_call(
        _mm_res_kernel,
        grid=(S // tm, N // tn),
        in_specs=[pl.BlockSpec((tm, K), lambda i, j: (i, 0)),
                  pl.BlockSpec((K, tn), lambda i, j: (0, j)),
                  pl.BlockSpec((tm, tn), lambda i, j: (i, j))],
        out_specs=pl.BlockSpec((tm, tn), lambda i, j: (i, j)),
        out_shape=jax.ShapeDtypeStruct((S, N), F32),
        compiler_params=_params("parallel", "arbitrary"),
        name="matmul_residual",
    )(a, w, h)


def _mlp_kernel(h_ref, g_ref, wu_ref, wd_ref, gf_ref, o_ref, xn_ref, *, final_norm):
    f = pl.program_id(1)

    @pl.when(f == 0)
    def _():
        h = h_ref[...]
        xn_ref[...] = _rmsnorm_rows(h, g_ref[...]).astype(BF16)
        o_ref[...] = h

    hid = jnp.maximum(_dot(xn_ref[...], wu_ref[...]), 0.0)
    o_ref[...] += _dot((hid * hid).astype(BF16), wd_ref[...])

    if final_norm:
        @pl.when(f == pl.num_programs(1) - 1)
        def _():
            o_ref[...] = _rmsnorm_rows(o_ref[...], gf_ref[...])


def mlp_residual(h, g, w_up, w_down, g_final=None, *, tm=512, tf=512):
    S, D = h.shape
    Fd = w_up.shape[1]
    tm, tf = min(tm, S), min(tf, Fd)
    assert S % tm == 0 and Fd % tf == 0
    final_norm = g_final is not None
    gf = (g_final if final_norm else g).reshape(1, D)
    return pl.pallas_call(
        functools.partial(_mlp_kernel, final_norm=final_norm),
        grid=(S // tm, Fd // tf),
        in_specs=[pl.BlockSpec((tm, D), lambda i, f: (i, 0)),
                  pl.BlockSpec((1, D), lambda i, f: (0, 0)),
                  pl.BlockSpec((D, tf), lambda i, f: (0, f)),
                  pl.BlockSpec((tf, D), lambda i, f: (f, 0)),
                  pl.BlockSpec((1, D), lambda i, f: (0, 0))],
        out_specs=pl.BlockSpec((tm, D), lambda i, f: (i, 0)),
        out_shape=jax.ShapeDtypeStruct((S, D), F32),
        scratch_shapes=[pltpu.VMEM((tm, D), BF16)],
        compiler_params=_params("parallel", "arbitrary"),
        name="mlp_residual",
    )(h, g.reshape(1, D), w_up, w_down, gf)


def _moba_keys_kernel(k_ref, kaug_ref, kmean_ref, *, nb):
    blk = MOBA_BLOCK
    lane = lax.broadcasted_iota(jnp.int32, (blk, LANES), 1)
    kmean_ref[...] = jnp.zeros(kmean_ref.shape, F32)
    for n in range(nb):
        rows = slice(n * blk, (n + 1) * blk)
        kn = k_ref[rows, :]
        kaug_ref[rows, 0:HEAD_DIM] = kn
        kaug_ref[rows, HEAD_DIM:2 * HEAD_DIM] = jnp.where(lane == n, 1.0, 0.0).astype(BF16)
        kmean_ref[n:n + 1, :] = jnp.mean(kn.astype(F32), axis=0, keepdims=True)


def _moba_gate_kernel(q_ref, kmean_ref, qaug_ref, *, nb, tq):
    blk = MOBA_BLOCK
    i = pl.program_id(1)
    km = kmean_ref[...]
    km_hi = km.astype(BF16)
    r1 = km - km_hi.astype(F32)
    km_mid = r1.astype(BF16)
    km_lo = (r1 - km_mid.astype(F32)).astype(BF16)
    blk_id = lax.broadcasted_iota(jnp.int32, (blk, LANES), 1)
    for sub in range(tq // blk):
        rows = slice(sub * blk, (sub + 1) * blk)
        qb = i * (tq // blk) + sub
        q = q_ref[rows, :]
        gate = _dot_nt(q, km_hi) + _dot_nt(q, km_mid) + _dot_nt(q, km_lo)
        gate = jnp.where(blk_id < qb, gate, NEG_INF)
        chosen = blk_id >= nb
        for n in range(min(MOBA_TOP_K, nb)):
            best = jnp.max(gate, axis=1, keepdims=True)
            idx = jnp.min(jnp.where(gate == best, blk_id, LANES), axis=1, keepdims=True)
            hit = blk_id == idx
            chosen = chosen | (hit & (n < qb))
            gate = jnp.where(hit, -jnp.inf, gate)
        qaug_ref[rows, 0:HEAD_DIM] = q
        qaug_ref[rows, HEAD_DIM:2 * HEAD_DIM] = jnp.where(chosen, 0.0, NEG_INF).astype(BF16)


def _moba_attn_kernel(qa_ref, ka_ref, v_ref, o_ref, s_ref, own_ref, m_ref, l_ref, acc_ref, *, chunk):
    blk = MOBA_BLOCK
    span = chunk * blk
    qb = pl.program_id(1)
    scale = HEAD_DIM ** -0.5
    start = pl.multiple_of(qb * blk, blk)
    n_chunks = (qb + chunk - 1) // chunk

    row = lax.broadcasted_iota(jnp.int32, (blk, blk), 0)
    col = lax.broadcasted_iota(jnp.int32, (blk, blk), 1)
    s_own = _dot_nt(qa_ref[:, 0:HEAD_DIM], ka_ref[pl.ds(start, blk), 0:HEAD_DIM]) * scale
    s_own = jnp.where(col <= row, s_own, NEG_INF)
    own_ref[...] = s_own
    m_run = functools.reduce(jnp.maximum, _lane_groups(s_own))

    def score(c, m_run):
        st = pl.multiple_of(c * span, span)
        s = _dot_nt(qa_ref[...], ka_ref[pl.ds(st, span), :]) * scale
        s_ref[c] = s
        return functools.reduce(jnp.maximum, _lane_groups(s), m_run)

    m_run = lax.fori_loop(0, n_chunks, score, m_run)
    m_ref[...] = jnp.broadcast_to(jnp.max(m_run, axis=1, keepdims=True), (blk, LANES))

    p_own = [jnp.exp(s - m_ref[...]) for s in _lane_groups(own_ref[...])]
    l_ref[...] = functools.reduce(jnp.add, p_own)
    acc_ref[...] = _dot(jnp.concatenate(p_own, axis=1).astype(BF16), v_ref[pl.ds(start, blk), :])

    def weigh(c, carry):
        st = pl.multiple_of(c * span, span)
        m = m_ref[...]
        p = [jnp.exp(s - m) for s in _lane_groups(s_ref[c])]
        l_ref[...] += functools.reduce(jnp.add, p)
        acc_ref[...] += _dot(jnp.concatenate(p, axis=1).astype(BF16), v_ref[pl.ds(st, span), :])
        return carry

    lax.fori_loop(0, n_chunks, weigh, 0)
    l = jnp.sum(l_ref[...], axis=1, keepdims=True)
    o_ref[...] = (acc_ref[...] * (1.0 / l)).astype(o_ref.dtype)


def moba_attention(qkv, n_heads, *, tq=1024):
    S = qkv.shape[0]
    blk, hd, H, chunk = MOBA_BLOCK, HEAD_DIM, n_heads, MOBA_CHUNK
    tq = min(tq, S)
    assert S % tq == 0 and tq % blk == 0 and hd == LANES
    nb = S // blk
    assert nb <= LANES and nb % chunk == 0
    kaug, kmean = pl.pallas_call(
        functools.partial(_moba_keys_kernel, nb=nb),
        grid=(H,),
        in_specs=[pl.BlockSpec((S, hd), lambda h: (0, H + h))],
        out_specs=[pl.BlockSpec((S, 2 * hd), lambda h: (0, h)),
                   pl.BlockSpec((LANES, hd), lambda h: (h, 0))],
        out_shape=[jax.ShapeDtypeStruct((S, H * 2 * hd), BF16),
                   jax.ShapeDtypeStruct((H * LANES, hd), F32)],
        compiler_params=_params("parallel"),
        name="moba_keys",
    )(qkv)
    qaug = pl.pallas_call(
        functools.partial(_moba_gate_kernel, nb=nb, tq=tq),
        grid=(H, S // tq),
        in_specs=[pl.BlockSpec((tq, hd), lambda h, i: (i, h)),
                  pl.BlockSpec((LANES, hd), lambda h, i: (h, 0))],
        out_specs=pl.BlockSpec((tq, 2 * hd), lambda h, i: (i, h)),
        out_shape=jax.ShapeDtypeStruct((S, H * 2 * hd), BF16),
        compiler_params=_params("parallel", "arbitrary"),
        name="moba_gate",
    )(qkv, kmean)
    return pl.pallas_call(
        functools.partial(_moba_attn_kernel, chunk=chunk),
        grid=(H, nb),
        in_specs=[pl.BlockSpec((blk, 2 * hd), lambda h, i: (i, h)),
                  pl.BlockSpec((S, 2 * hd), lambda h, i: (0, h)),
                  pl.BlockSpec((S, hd), lambda h, i: (0, 2 * H + h))],
        out_specs=pl.BlockSpec((blk, hd), lambda h, i: (i, h)),
        out_shape=jax.ShapeDtypeStruct((S, H * hd), BF16),
        scratch_shapes=[pltpu.VMEM((nb // chunk, blk, chunk * blk), F32),
                        pltpu.VMEM((blk, blk), F32),
                        pltpu.VMEM((blk, LANES), F32),
                        pltpu.VMEM((blk, LANES), F32),
                        pltpu.VMEM((blk, hd), F32)],
        compiler_params=_params("parallel", "arbitrary"),
        name="moba_attention",
    )(qaug, kaug, qkv)


def _conv_in_kernel(x_ref, g_ref, wb_ref, wc_ref, wu_ref, b_ref, z_ref, xn_ref):
    @pl.when(pl.program_id(1) == 0)
    def _():
        xn_ref[...] = _rmsnorm_rows(x_ref[...], g_ref[...]).astype(BF16)

    xn = xn_ref[...]
    b_ref[...] = _dot(xn, wb_ref[...])
    z_ref[...] = _dot(xn, wc_ref[...]) * _dot(xn, wu_ref[...])


def conv_in(x, g, w_in, *, tm=512, tn=512):
    S, D = x.shape
    tm, tn = min(tm, S), min(tn, D)
    assert S % tm == 0 and D % tn == 0 and w_in.shape[1] == 3 * D
    nj = D // tn
    out = jax.ShapeDtypeStruct((S, D), F32)
    return pl.pallas_call(
        _conv_in_kernel,
        grid=(S // tm, nj),
        in_specs=[pl.BlockSpec((tm, D), lambda i, j: (i, 0)),
                  pl.BlockSpec((1, D), lambda i, j: (0, 0)),
                  pl.BlockSpec((D, tn), lambda i, j: (0, j)),
                  pl.BlockSpec((D, tn), lambda i, j: (0, nj + j)),
                  pl.BlockSpec((D, tn), lambda i, j: (0, 2 * nj + j))],
        out_specs=[pl.BlockSpec((tm, tn), lambda i, j: (i, j))] * 2,
        out_shape=[out, out],
        scratch_shapes=[pltpu.VMEM((tm, D), BF16)],
        compiler_params=_params("parallel", "arbitrary"),
        name="conv_in",
    )(x, g.reshape(1, D), w_in, w_in, w_in)


HALO = 8


def _conv_out_kernel(b_ref, z_ref, zh_ref, cw_ref, w_ref, h_ref, o_ref, a_ref):
    @pl.when(pl.program_id(1) == 0)
    def _():
        z = z_ref[...]
        first = pl.program_id(0) == 0
        halo = jnp.where(first, 0.0, zh_ref[...])
        row = lax.broadcasted_iota(jnp.int32, z.shape, 0)
        zm1 = jnp.where(row == 0, halo[HALO - 1:HALO, :], pltpu.roll(z, 1, 0))
        zm2 = jnp.where(row == 0, halo[HALO - 2:HALO - 1, :],
                        jnp.where(row == 1, halo[HALO - 1:HALO, :], pltpu.roll(z, 2, 0)))
        cw = cw_ref[...]
        conv = cw[0:1, :] * zm2 + cw[1:2, :] * zm1 + cw[2:3, :] * z
        a_ref[...] = (b_ref[...] * conv).astype(BF16)

    o_ref[...] = h_ref[...] + _dot(a_ref[...], w_ref[...])


def conv_out(b, z, conv_w, w_out, h, *, tm=512, tn=1024):
    S, D = z.shape
    N = w_out.shape[1]
    tm, tn = min(tm, S), min(tn, N)
    assert S % tm == 0 and N % tn == 0 and tm % HALO == 0 and conv_w.shape[0] == CONV_WIDTH
    return pl.pallas_call(
        _conv_out_kernel,
        grid=(S // tm, N // tn),
        in_specs=[pl.BlockSpec((tm, D), lambda i, j: (i, 0)),
                  pl.BlockSpec((tm, D), lambda i, j: (i, 0)),
                  pl.BlockSpec((HALO, D), lambda i, j: (jnp.maximum(i * (tm // HALO) - 1, 0), 0)),
                  pl.BlockSpec((CONV_WIDTH, D), lambda i, j: (0, 0)),
                  pl.BlockSpec((D, tn), lambda i, j: (0, j)),
                  pl.BlockSpec((tm, tn), lambda i, j: (i, j))],
        out_specs=pl.BlockSpec((tm, tn), lambda i, j: (i, j)),
        out_shape=jax.ShapeDtypeStruct((S, N), F32),
        scratch_shapes=[pltpu.VMEM((tm, D), BF16)],
        compiler_params=_params("parallel", "arbitrary"),
        name="conv_out",
    )(b, z, z, conv_w, w_out, h)


def _dilated_kernel(q_ref, kp_ref, ko_ref, vp_ref, vo_ref, o_ref, lse_ref, *, band, n_heads):
    j = pl.program_id(1)
    scale = HEAD_DIM ** -0.5
    a = lax.broadcasted_iota(jnp.int32, (band, 2 * band), 0)
    b = lax.broadcasted_iota(jnp.int32, (band, 2 * band), 1)
    valid = (b >= a) & (b <= a + band) & ((j > 0) | (b >= band))
    for hh in range(n_heads):
        sl = slice(hh * HEAD_DIM, (hh + 1) * HEAD_DIM)
        kband = jnp.concatenate([kp_ref[:, sl], ko_ref[:, sl]], axis=0)
        vband = jnp.concatenate([vp_ref[:, sl], vo_ref[:, sl]], axis=0)
        s = _dot_nt(q_ref[:, sl], kband) * scale
        s = jnp.where(valid, s, NEG_INF)
        m = jnp.max(s, axis=1, keepdims=True)
        e = jnp.exp(s - m)
        l = jnp.sum(e, axis=1, keepdims=True)
        o_ref[:, sl] = _dot(e.astype(BF16), vband) * (1.0 / l)
        lse_ref[:, sl] = jnp.broadcast_to(m + jnp.log(l), (band, HEAD_DIM))


def dilated_group_attention(qkv, window, dilation, n_heads):
    dil, L, W = qkv.shape
    band = window // dilation
    width = n_heads * HEAD_DIM
    assert dil == dilation and L % band == 0 and W == 3 * width
    nbl = L // band

    def spec(t, prev):
        if prev:
            return pl.BlockSpec((None, band, width), lambda r, j: (r, jnp.maximum(j - 1, 0), t))
        return pl.BlockSpec((None, band, width), lambda r, j: (r, j, t))

    out = jax.ShapeDtypeStruct((dil, L, width), F32)
    return pl.pallas_call(
        functools.partial(_dilated_kernel, band=band, n_heads=n_heads),
        grid=(dil, nbl),
        in_specs=[spec(0, False), spec(1, True), spec(1, False), spec(2, True), spec(2, False)],
        out_specs=[pl.BlockSpec((None, band, width), lambda r, j: (r, j, 0))] * 2,
        out_shape=[out, out],
        compiler_params=_params("parallel", "arbitrary"),
        name="dilated_attention",
    )(qkv, qkv, qkv, qkv, qkv)


def _combine_kernel(*refs, dilations):
    n = len(dilations)
    o_refs, l_refs, out_ref = refs[:n], refs[n:2 * n], refs[2 * n]
    scratch = list(refs[2 * n + 1:])
    tm, width = out_ref.shape

    views = []
    for g, dil in enumerate(dilations):
        if dil == 1:
            views.append((lambda sl, o=o_refs[g]: o[0, :, sl], lambda sl, l=l_refs[g]: l[0, :, sl]))
            continue
        o_scr, l_scr = scratch.pop(0), scratch.pop(0)
        rows = tm // dil
        for c in range(width // LANES):
            sl = slice(c * LANES, (c + 1) * LANES)
            for r in range(dil):
                o_scr[c, pl.ds(r, rows, stride=dil), :] = o_refs[g][r, :, sl]
                l_scr[c, pl.ds(r, rows, stride=dil), :] = l_refs[g][r, :, sl]
        views.append((lambda sl, s=o_scr: s[sl.start // LANES], lambda sl, s=l_scr: s[sl.start // LANES]))

    for c in range(width // LANES):
        sl = slice(c * LANES, (c + 1) * LANES)
        lses = [lv(sl) for _, lv in views]
        top = functools.reduce(jnp.maximum, lses)
        ws = [jnp.exp(l - top) for l in lses]
        inv = 1.0 / functools.reduce(jnp.add, ws)
        acc = (ws[0] * inv) * views[0][0](sl)
        for w, (ov, _) in zip(ws[1:], views[1:]):
            acc = acc + (w * inv) * ov(sl)
        out_ref[:, sl] = acc.astype(out_ref.dtype)


def combine_groups(outs, lses, dilations, *, tm=256):
    width = outs[0].shape[2]
    S = outs[0].shape[0] * outs[0].shape[1]
    tm = min(tm, S)
    assert S % tm == 0 and all(tm % (8 * d) == 0 for d in dilations)
    specs = [pl.BlockSpec((d, tm // d, width), lambda i: (0, i, 0)) for d in dilations]
    scratch = []
    for d in dilations:
        if d > 1:
            scratch += [pltpu.VMEM((width // LANES, tm, LANES), F32)] * 2
    return pl.pallas_call(
        functools.partial(_combine_kernel, dilations=tuple(dilations)),
        grid=(S // tm,),
        in_specs=specs + specs,
        out_specs=pl.BlockSpec((tm, width), lambda i: (i, 0)),
        out_shape=jax.ShapeDtypeStruct((S, width), BF16),
        scratch_shapes=scratch,
        compiler_params=_params("parallel"),
        name="combine_groups",
    )(*outs, *lses)


def kernel(x, positions, norm_mix, norm_mlp, norm_final, mlp_w_up, mlp_w_down, moba_w_qkv, moba_w_o,
           conv_w_in, conv_w, conv_w_out, dil_w_qkv, dil_w_o):
    B, S, D = x.shape
    assert B == 1 and D % HEAD_DIM == 0
    n_heads = D // HEAD_DIM
    depth = norm_mix.shape[0]
    h = x[0]
    tables = rope_tables(positions[0])
    for i in range(depth):
        kind, j = i % N_MIXERS, i // N_MIXERS
        if kind == 0:
            qkv = norm_matmul_rope(h, norm_mix[i], moba_w_qkv[j].astype(BF16), tables)
            o = moba_attention(qkv[0], n_heads)
            h = matmul_residual(o, moba_w_o[j].astype(BF16), h)
        elif kind == 1:
            b, z = conv_in(h, norm_mix[i], conv_w_in[j].astype(BF16))
            h = conv_out(b, z, conv_w[j], conv_w_out[j].astype(BF16), h)
        else:
            w_qkv = dil_w_qkv[j].astype(BF16)
            outs, lses = [], []
            for g, (window, dilation) in enumerate(DIL_GROUPS):
                w_g = w_qkv[:, g * 3 * D:(g + 1) * 3 * D]
                qkv = norm_matmul_rope(h, norm_mix[i], w_g, tables, dilation=dilation)
                o_g, lse_g = dilated_group_attention(qkv, window, dilation, n_heads)
                outs.append(o_g)
                lses.append(lse_g)
            o = combine_groups(outs, lses, [d for _, d in DIL_GROUPS])
            h = matmul_residual(o, dil_w_o[j].astype(BF16), h)
        g_final = norm_final if i == depth - 1 else None
        h = mlp_residual(h, norm_mlp[i], mlp_w_up[i].astype(BF16), mlp_w_down[i].astype(BF16), g_final)
    return h[None]
```

```python
import functools
import math

import jax
import jax.numpy as jnp
from jax import lax
from jax.experimental import pallas as pl
from jax.experimental.pallas import tpu as pltpu

HEAD_DIM = 128
MOBA_BLOCK = 256
MOBA_TOP_K = 3
MOBA_CHUNK = 4
CONV_WIDTH = 3
DIL_GROUPS = ((128, 1), (512, 4), (2048, 16))
ROT_DIM = HEAD_DIM // 4
ROPE_THETA = 500000.0
RMS_EPS = 1e-5
NEG_INF = -1e30
N_MIXERS = 3

LANES = 128
VMEM_LIMIT = 56 * 1024 * 1024

F32 = jnp.float32
BF16 = jnp.bfloat16


def _params(*sem):
    return pltpu.CompilerParams(dimension_semantics=sem, vmem_limit_bytes=VMEM_LIMIT)


def _rmsnorm_rows(x, g):
    y = x * lax.rsqrt(jnp.mean(x * x, axis=-1, keepdims=True) + RMS_EPS)
    return y * g


def _dot(a, b):
    return jnp.dot(a, b, preferred_element_type=F32)


def _dot_nt(a, b):
    return lax.dot_general(a, b, (((1,), (1,)), ((), ())), preferred_element_type=F32)


def _dot_tn(a, b):
    return lax.dot_general(a, b, (((0,), (0,)), ((), ())), preferred_element_type=F32)


def _lane_groups(x):
    return [x[:, g * LANES:(g + 1) * LANES] for g in range(x.shape[1] // LANES)]


def _rope_table_kernel(pos_ref, invf_ref, cos_ref, sa_ref, sb_ref):
    ang = pos_ref[...] * invf_ref[...]
    lane = lax.broadcasted_iota(jnp.int32, ang.shape, 1)
    c, s = jnp.cos(ang), jnp.sin(ang)
    half = ROT_DIM // 2
    cos_ref[...] = jnp.where(lane < ROT_DIM, c, 1.0)
    sa_ref[...] = jnp.where(lane < half, -s, 0.0)
    sb_ref[...] = jnp.where((lane >= half) & (lane < ROT_DIM), s, 0.0)


def rope_tables(positions):
    S = positions.shape[0]
    tm = min(S, 1024)
    half = ROT_DIM // 2
    inv_freq = ROPE_THETA ** (-jnp.arange(0, ROT_DIM, 2, dtype=F32) / ROT_DIM)
    invf = jnp.concatenate([inv_freq, inv_freq, jnp.zeros((LANES - 2 * half,), F32)])[None, :]
    pos = positions.astype(F32)[:, None]
    tab = jax.ShapeDtypeStruct((S, LANES), F32)
    return pl.pallas_call(
        _rope_table_kernel,
        grid=(S // tm,),
        in_specs=[pl.BlockSpec((tm, 1), lambda i: (i, 0)),
                  pl.BlockSpec((1, LANES), lambda i: (0, 0))],
        out_specs=[pl.BlockSpec((tm, LANES), lambda i: (i, 0))] * 3,
        out_shape=[tab, tab, tab],
        compiler_params=_params("arbitrary"),
        name="rope_tables",
    )(pos, invf)


def _apply_rope(t, cos, sa, sb):
    half = ROT_DIM // 2
    return (t * cos + pltpu.roll(t, LANES - half, 1) * sa + pltpu.roll(t, half, 1) * sb)


def _norm_mm_rope_kernel(x_ref, g_ref, w_ref, cos_ref, sa_ref, sb_ref, o_ref, xn_ref, *scratch,
                         tn, d_model, dilation):
    j = pl.program_id(1)

    @pl.when(j == 0)
    def _():
        xn_ref[...] = _rmsnorm_rows(x_ref[...], g_ref[...]).astype(BF16)

    acc = _dot(xn_ref[...], w_ref[...])
    roped = ((j * tn) // d_model) % 3 != 2
    heads = tn // HEAD_DIM
    if dilation == 1:
        cos = jnp.where(roped, cos_ref[...], 1.0)
        sa = jnp.where(roped, sa_ref[...], 0.0)
        sb = jnp.where(roped, sb_ref[...], 0.0)
        for c in range(heads):
            sl = slice(c * HEAD_DIM, (c + 1) * HEAD_DIM)
            o_ref[0, :, sl] = _apply_rope(acc[:, sl], cos, sa, sb).astype(o_ref.dtype)
    else:
        acc_ref, = scratch
        for c in range(heads):
            acc_ref[c] = acc[:, c * HEAD_DIM:(c + 1) * HEAD_DIM]
        rows = acc.shape[0] // dilation
        for r in range(dilation):
            pick = pl.ds(r, rows, stride=dilation)
            cos = jnp.where(roped, cos_ref[pick, :], 1.0)
            sa = jnp.where(roped, sa_ref[pick, :], 0.0)
            sb = jnp.where(roped, sb_ref[pick, :], 0.0)
            for c in range(heads):
                sl = slice(c * HEAD_DIM, (c + 1) * HEAD_DIM)
                o_ref[r, :, sl] = _apply_rope(acc_ref[c, pick, :], cos, sa, sb).astype(o_ref.dtype)


def norm_matmul_rope(x, g, w_all, layer, tables, *, n_out, col0=0, dilation=1, tm=1024, tn=1024):
    S, D = x.shape
    tm = min(tm, S)
    assert S % tm == 0 and n_out % tn == 0 and D % tn == 0 and tm % (16 * dilation) == 0
    rows = tm // dilation
    scratch = [pltpu.VMEM((tm, D), BF16)]
    if dilation > 1:
        scratch.append(pltpu.VMEM((tn // HEAD_DIM, tm, HEAD_DIM), F32))
    return pl.pallas_call(
        functools.partial(_norm_mm_rope_kernel, tn=tn, d_model=D, dilation=dilation),
        grid=(S // tm, n_out // tn),
        in_specs=[pl.BlockSpec((tm, D), lambda i, j: (i, 0)),
                  pl.BlockSpec((1, D), lambda i, j: (0, 0)),
                  pl.BlockSpec((None, D, tn), lambda i, j: (layer, 0, col0 + j))]
                 + [pl.BlockSpec((tm, LANES), lambda i, j: (i, 0))] * 3,
        out_specs=pl.BlockSpec((dilation, rows, tn), lambda i, j: (0, i, j)),
        out_shape=jax.ShapeDtypeStruct((dilation, S // dilation, n_out), BF16),
        scratch_shapes=scratch,
        compiler_params=_params("parallel", "arbitrary"),
        name="norm_matmul_rope",
    )(x, g.reshape(1, D), w_all, *tables)


def _mm_res_kernel(*refs, n_parts, tiles_per_part):
    a_refs, (w_ref, h_ref, o_ref) = refs[:n_parts], refs[n_parts:]
    part = pl.program_id(0) // tiles_per_part
    for p, a_ref in enumerate(a_refs):
        @pl.when(part == p)
        def _(a_ref=a_ref):
            o_ref[...] = h_ref[...] + _dot(a_ref[...], w_ref[...])


def matmul_residual(a_parts, w_all, layer, h, *, tm=1024, tn=1024):
    n_parts = len(a_parts)
    Sp, K = a_parts[0].shape
    S, N = h.shape
    tm, tn = min(tm, Sp), min(tn, N)
    assert Sp * n_parts == S and Sp % tm == 0 and N % tn == 0
    tiles = Sp // tm

    def a_spec(p):
        return pl.BlockSpec((tm, K), lambda i, j: (jnp.clip(i - p * tiles, 0, tiles - 1), 0))

    return pl.pallas_call(
        functools.partial(_mm_res_kernel, n_parts=n_parts, tiles_per_part=tiles),
        grid=(S // tm, N // tn),
        in_specs=[a_spec(p) for p in range(n_parts)]
                 + [pl.BlockSpec((None, K, tn), lambda i, j: (layer, 0, j)),
                    pl.BlockSpec((tm, tn), lambda i, j: (i, j))],
        out_specs=pl.BlockSpec((tm, tn), lambda i, j: (i, j)),
        out_shape=jax.ShapeDtypeStruct((S, N), F32),
        compiler_params=_params("parallel", "arbitrary"),
        name="matmul_residual",
    )(*a_parts, w_all, h)


def _mlp_kernel(h_ref, g_ref, wu_ref, wd_ref, gf_ref, o_ref, xn_ref, *, final_norm):
    f = pl.program_id(1)

    @pl.when(f == 0)
    def _():
        h = h_ref[...]
        xn_ref[...] = _rmsnorm_rows(h, g_ref[...]).astype(BF16)
        o_ref[...] = h

    hid = jnp.maximum(_dot(xn_ref[...], wu_ref[...]), 0.0)
    o_ref[...] += _dot((hid * hid).astype(BF16), wd_ref[...])

    if final_norm:
        @pl.when(f == pl.num_programs(1) - 1)
        def _():
            o_ref[...] = _rmsnorm_rows(o_ref[...], gf_ref[...])


def mlp_residual(h, g, w_up_all, w_down_all, layer, g_final=None, *, tm=512, tf=512):
    S, D = h.shape
    Fd = w_up_all.shape[2]
    tm, tf = min(tm, S), min(tf, Fd)
    assert S % tm == 0 and Fd % tf == 0
    final_norm = g_final is not None
    gf = (g_final if final_norm else g).reshape(1, D)
    return pl.pallas_call(
        functools.partial(_mlp_kernel, final_norm=final_norm),
        grid=(S // tm, Fd // tf),
        in_specs=[pl.BlockSpec((tm, D), lambda i, f: (i, 0)),
                  pl.BlockSpec((1, D), lambda i, f: (0, 0)),
                  pl.BlockSpec((None, D, tf), lambda i, f: (layer, 0, f)),
                  pl.BlockSpec((None, tf, D), lambda i, f: (layer, f, 0)),
                  pl.BlockSpec((1, D), lambda i, f: (0, 0))],
        out_specs=pl.BlockSpec((tm, D), lambda i, f: (i, 0)),
        out_shape=jax.ShapeDtypeStruct((S, D), F32),
        scratch_shapes=[pltpu.VMEM((tm, D), BF16)],
        compiler_params=_params("parallel", "arbitrary"),
        name="mlp_residual",
    )(h, g.reshape(1, D), w_up_all, w_down_all, gf)


def _moba_keys_kernel(k_ref, v_ref, kaug_ref, vaug_ref, kmean_ref, *, nb):
    blk = MOBA_BLOCK
    lane = lax.broadcasted_iota(jnp.int32, (blk, LANES), 1)
    ones_col = jnp.where(lane == 0, 1.0, 0.0).astype(BF16)
    kmean_ref[...] = jnp.zeros(kmean_ref.shape, F32)
    for n in range(nb):
        rows = slice(n * blk, (n + 1) * blk)
        kn = k_ref[rows, :]
        kaug_ref[rows, 0:HEAD_DIM] = kn
        kaug_ref[rows, HEAD_DIM:2 * HEAD_DIM] = jnp.where(lane == n, 1.0, 0.0).astype(BF16)
        vaug_ref[rows, 0:HEAD_DIM] = v_ref[rows, :]
        vaug_ref[rows, HEAD_DIM:2 * HEAD_DIM] = ones_col
        kmean_ref[n:n + 1, :] = jnp.mean(kn.astype(F32), axis=0, keepdims=True)


def _moba_gate_kernel(q_ref, kmean_ref, qaug_ref, *, nb, tq):
    blk = MOBA_BLOCK
    i = pl.program_id(1)
    q = q_ref[...]
    km = kmean_ref[0:nb, :]
    km_hi = km.astype(BF16)
    r1 = km - km_hi.astype(F32)
    km_mid = r1.astype(BF16)
    km_lo = (r1 - km_mid.astype(F32)).astype(BF16)
    gate = _dot_nt(km_hi, q) + _dot_nt(km_mid, q) + _dot_nt(km_lo, q)

    blk_id = lax.broadcasted_iota(jnp.int32, (nb, tq), 0)
    tok = lax.broadcasted_iota(jnp.int32, (nb, tq), 1)
    qb = i * (tq // blk) + lax.shift_right_logical(tok, jnp.int32(blk.bit_length() - 1))
    gate = jnp.where(blk_id < qb, gate, NEG_INF)
    chosen = jnp.zeros((nb, tq), jnp.bool_)
    for n in range(min(MOBA_TOP_K, nb)):
        best = jnp.max(gate, axis=0, keepdims=True)
        idx = jnp.min(jnp.where(gate == best, blk_id, nb), axis=0, keepdims=True)
        hit = blk_id == idx
        chosen = chosen | (hit & (n < qb))
        gate = jnp.where(hit, -jnp.inf, gate)

    eye = (lax.broadcasted_iota(jnp.int32, (nb, LANES), 0)
           == lax.broadcasted_iota(jnp.int32, (nb, LANES), 1))
    picked = _dot_tn(jnp.where(chosen, 1.0, 0.0).astype(BF16), jnp.where(eye, 1.0, 0.0).astype(BF16))
    lane = lax.broadcasted_iota(jnp.int32, (tq, LANES), 1)
    qaug_ref[:, 0:HEAD_DIM] = q
    qaug_ref[:, HEAD_DIM:2 * HEAD_DIM] = jnp.where((picked > 0.5) | (lane >= nb), 0.0, NEG_INF).astype(BF16)


def _moba_attn_kernel(qlo_ref, qhi_ref, ka_ref, va_ref, olo_ref, ohi_ref, s_ref, own_ref, m_ref, acc_ref,
                      *, chunk, nb):
    blk = MOBA_BLOCK
    span = chunk * blk
    last_chunk = nb // chunk - 1
    n_slots = nb // chunk + 1
    p = pl.program_id(1)
    qbs = (p, nb - 1 - p)
    q_refs, o_refs = (qlo_ref, qhi_ref), (olo_ref, ohi_ref)
    log2_scale = HEAD_DIM ** -0.5 * math.log2(math.e)
    n_lo = (qbs[0] + chunk - 1) // chunk
    n_hi = (qbs[1] + chunk - 1) // chunk

    def slot(c):
        is_hi = c < n_hi
        is_lo = jnp.logical_and(c >= n_hi, c < n_hi + n_lo)
        ck = jnp.where(is_hi, c, jnp.where(is_lo, c - n_hi, last_chunk))
        return is_hi, jnp.where(is_hi, 1, 0), pl.multiple_of(ck * span, span)

    row = lax.broadcasted_iota(jnp.int32, (blk, blk), 0)
    col = lax.broadcasted_iota(jnp.int32, (blk, blk), 1)
    starts = [pl.multiple_of(qb * blk, blk) for qb in qbs]
    for t in range(2):
        s = _dot_nt(q_refs[t][:, 0:HEAD_DIM], ka_ref[pl.ds(starts[t], blk), 0:HEAD_DIM]) * log2_scale
        s = jnp.where(col <= row, s, NEG_INF)
        own_ref[t] = s
        m_ref[t] = functools.reduce(jnp.maximum, _lane_groups(s))
    for c in range(n_slots):
        is_hi, which, st = slot(c)
        qa = jnp.where(is_hi, qhi_ref[...], qlo_ref[...])
        s = _dot_nt(qa, ka_ref[pl.ds(st, span), :]) * log2_scale
        s_ref[c] = s
        m_ref[which] = functools.reduce(jnp.maximum, _lane_groups(s), m_ref[which])
    for t in range(2):
        m_ref[t] = jnp.broadcast_to(jnp.max(m_ref[t], axis=1, keepdims=True), (blk, LANES))

    for t in range(2):
        m = m_ref[t]
        pr = [jnp.exp2(s - m) for s in _lane_groups(own_ref[t])]
        acc_ref[t] = _dot(jnp.concatenate(pr, axis=1).astype(BF16), va_ref[pl.ds(starts[t], blk), :])
    for c in range(n_slots):
        _, which, st = slot(c)
        m = m_ref[which]
        pr = [jnp.exp2(s - m) for s in _lane_groups(s_ref[c])]
        acc_ref[which] += _dot(jnp.concatenate(pr, axis=1).astype(BF16), va_ref[pl.ds(st, span), :])
    for t in range(2):
        acc = acc_ref[t]
        o_refs[t][...] = (acc[:, 0:HEAD_DIM] * (1.0 / acc[:, HEAD_DIM:HEAD_DIM + 1])).astype(o_refs[t].dtype)


def moba_attention(qkv, n_heads, *, tq=1024):
    S = qkv.shape[0]
    blk, hd, H, chunk = MOBA_BLOCK, HEAD_DIM, n_heads, MOBA_CHUNK
    tq = min(tq, S)
    assert S % tq == 0 and tq % blk == 0 and hd == LANES and blk & (blk - 1) == 0
    nb = S // blk
    assert nb <= LANES and nb % 8 == 0 and nb % (2 * chunk) == 0
    half = nb // 2
    kaug, vaug, kmean = pl.pallas_call(
        functools.partial(_moba_keys_kernel, nb=nb),
        grid=(H,),
        in_specs=[pl.BlockSpec((S, hd), lambda h: (0, H + h)),
                  pl.BlockSpec((S, hd), lambda h: (0, 2 * H + h))],
        out_specs=[pl.BlockSpec((S, 2 * hd), lambda h: (0, h)),
                   pl.BlockSpec((S, 2 * hd), lambda h: (0, h)),
                   pl.BlockSpec((LANES, hd), lambda h: (h, 0))],
        out_shape=[jax.ShapeDtypeStruct((S, H * 2 * hd), BF16),
                   jax.ShapeDtypeStruct((S, H * 2 * hd), BF16),
                   jax.ShapeDtypeStruct((H * LANES, hd), F32)],
        compiler_params=_params("parallel"),
        name="moba_keys",
    )(qkv, qkv)
    qaug = pl.pallas_call(
        functools.partial(_moba_gate_kernel, nb=nb, tq=tq),
        grid=(H, S // tq),
        in_specs=[pl.BlockSpec((tq, hd), lambda h, i: (i, h)),
                  pl.BlockSpec((LANES, hd), lambda h, i: (h, 0))],
        out_specs=pl.BlockSpec((tq, 2 * hd), lambda h, i: (i, h)),
        out_shape=jax.ShapeDtypeStruct((S, H * 2 * hd), BF16),
        compiler_params=_params("parallel", "arbitrary"),
        name="moba_gate",
    )(qkv, kmean)
    out = jax.ShapeDtypeStruct((S // 2, H * hd), BF16)
    return pl.pallas_call(
        functools.partial(_moba_attn_kernel, chunk=chunk, nb=nb),
        grid=(H, half),
        in_specs=[pl.BlockSpec((blk, 2 * hd), lambda h, p: (p, h)),
                  pl.BlockSpec((blk, 2 * hd), lambda h, p: (nb - 1 - p, h)),
                  pl.BlockSpec((S, 2 * hd), lambda h, p: (0, h)),
                  pl.BlockSpec((S, 2 * hd), lambda h, p: (0, h))],
        out_specs=[pl.BlockSpec((blk, hd), lambda h, p: (p, h)),
                   pl.BlockSpec((blk, hd), lambda h, p: (half - 1 - p, h))],
        out_shape=[out, out],
        scratch_shapes=[pltpu.VMEM((nb // chunk + 1, blk, chunk * blk), F32),
                        pltpu.VMEM((2, blk, blk), F32),
                        pltpu.VMEM((2, blk, LANES), F32),
                        pltpu.VMEM((2, blk, 2 * hd), F32)],
        compiler_params=_params("parallel", "arbitrary"),
        name="moba_attention",
    )(qaug, qaug, kaug, vaug)


def _conv_in_kernel(x_ref, g_ref, wb_ref, wc_ref, wu_ref, b_ref, z_ref, xn_ref):
    @pl.when(pl.program_id(1) == 0)
    def _():
        xn_ref[...] = _rmsnorm_rows(x_ref[...], g_ref[...]).astype(BF16)

    xn = xn_ref[...]
    b_ref[...] = _dot(xn, wb_ref[...])
    z_ref[...] = _dot(xn, wc_ref[...]) * _dot(xn, wu_ref[...])


def conv_in(x, g, w_in_all, layer, *, tm=512, tn=512):
    S, D = x.shape
    tm, tn = min(tm, S), min(tn, D)
    assert S % tm == 0 and D % tn == 0 and w_in_all.shape[2] == 3 * D
    nj = D // tn
    out = jax.ShapeDtypeStruct((S, D), F32)
    return pl.pallas_call(
        _conv_in_kernel,
        grid=(S // tm, nj),
        in_specs=[pl.BlockSpec((tm, D), lambda i, j: (i, 0)),
                  pl.BlockSpec((1, D), lambda i, j: (0, 0)),
                  pl.BlockSpec((None, D, tn), lambda i, j: (layer, 0, j)),
                  pl.BlockSpec((None, D, tn), lambda i, j: (layer, 0, nj + j)),
                  pl.BlockSpec((None, D, tn), lambda i, j: (layer, 0, 2 * nj + j))],
        out_specs=[pl.BlockSpec((tm, tn), lambda i, j: (i, j))] * 2,
        out_shape=[out, out],
        scratch_shapes=[pltpu.VMEM((tm, D), BF16)],
        compiler_params=_params("parallel", "arbitrary"),
        name="conv_in",
    )(x, g.reshape(1, D), w_in_all, w_in_all, w_in_all)


HALO = 8


def _conv_out_kernel(b_ref, z_ref, zh_ref, cw_ref, w_ref, h_ref, o_ref, a_ref):
    @pl.when(pl.program_id(1) == 0)
    def _():
        z = z_ref[...]
        first = pl.program_id(0) == 0
        halo = jnp.where(first, 0.0, zh_ref[...])
        row = lax.broadcasted_iota(jnp.int32, z.shape, 0)
        zm1 = jnp.where(row == 0, halo[HALO - 1:HALO, :], pltpu.roll(z, 1, 0))
        zm2 = jnp.where(row == 0, halo[HALO - 2:HALO - 1, :],
                        jnp.where(row == 1, halo[HALO - 1:HALO, :], pltpu.roll(z, 2, 0)))
        cw = cw_ref[...]
        conv = cw[0:1, :] * zm2 + cw[1:2, :] * zm1 + cw[2:3, :] * z
        a_ref[...] = (b_ref[...] * conv).astype(BF16)

    o_ref[...] = h_ref[...] + _dot(a_ref[...], w_ref[...])


def conv_out(b, z, conv_w, w_out_all, layer, h, *, tm=512, tn=1024):
    S, D = z.shape
    N = w_out_all.shape[2]
    tm, tn = min(tm, S), min(tn, N)
    assert S % tm == 0 and N % tn == 0 and tm % HALO == 0 and conv_w.shape[0] == CONV_WIDTH
    return pl.pallas_call(
        _conv_out_kernel,
        grid=(S // tm, N // tn),
        in_specs=[pl.BlockSpec((tm, D), lambda i, j: (i, 0)),
                  pl.BlockSpec((tm, D), lambda i, j: (i, 0)),
                  pl.BlockSpec((HALO, D), lambda i, j: (jnp.maximum(i * (tm // HALO) - 1, 0), 0)),
                  pl.BlockSpec((CONV_WIDTH, D), lambda i, j: (0, 0)),
                  pl.BlockSpec((None, D, tn), lambda i, j: (layer, 0, j)),
                  pl.BlockSpec((tm, tn), lambda i, j: (i, j))],
        out_specs=pl.BlockSpec((tm, tn), lambda i, j: (i, j)),
        out_shape=jax.ShapeDtypeStruct((S, N), F32),
        scratch_shapes=[pltpu.VMEM((tm, D), BF16)],
        compiler_params=_params("parallel", "arbitrary"),
        name="conv_out",
    )(b, z, z, conv_w, w_out_all, h)


def _dilated_kernel(q_ref, kp_ref, ko_ref, vp_ref, vo_ref, o_ref, lse_ref, *, band, n_heads):
    j = pl.program_id(1)
    scale = HEAD_DIM ** -0.5
    a = lax.broadcasted_iota(jnp.int32, (band, 2 * band), 0)
    b = lax.broadcasted_iota(jnp.int32, (band, 2 * band), 1)
    valid = (b >= a) & (b <= a + band) & ((j > 0) | (b >= band))
    for hh in range(n_heads):
        sl = slice(hh * HEAD_DIM, (hh + 1) * HEAD_DIM)
        kband = jnp.concatenate([kp_ref[:, sl], ko_ref[:, sl]], axis=0)
        vband = jnp.concatenate([vp_ref[:, sl], vo_ref[:, sl]], axis=0)
        s = _dot_nt(q_ref[:, sl], kband) * scale
        s = jnp.where(valid, s, NEG_INF)
        m = jnp.max(s, axis=1, keepdims=True)
        e = jnp.exp(s - m)
        l = jnp.sum(e, axis=1, keepdims=True)
        o_ref[:, sl] = _dot(e.astype(BF16), vband) * (1.0 / l)
        lse_ref[:, sl] = jnp.broadcast_to(m + jnp.log(l), (band, HEAD_DIM))


def dilated_group_attention(qkv, window, dilation, n_heads):
    dil, L, W = qkv.shape
    band = window // dilation
    width = n_heads * HEAD_DIM
    assert dil == dilation and L % band == 0 and W == 3 * width
    nbl = L // band

    def spec(t, prev):
        if prev:
            return pl.BlockSpec((None, band, width), lambda r, j: (r, jnp.maximum(j - 1, 0), t))
        return pl.BlockSpec((None, band, width), lambda r, j: (r, j, t))

    out = jax.ShapeDtypeStruct((dil, L, width), F32)
    return pl.pallas_call(
        functools.partial(_dilated_kernel, band=band, n_heads=n_heads),
        grid=(dil, nbl),
        in_specs=[spec(0, False), spec(1, True), spec(1, False), spec(2, True), spec(2, False)],
        out_specs=[pl.BlockSpec((None, band, width), lambda r, j: (r, j, 0))] * 2,
        out_shape=[out, out],
        compiler_params=_params("parallel", "arbitrary"),
        name="dilated_attention",
    )(qkv, qkv, qkv, qkv, qkv)


def _combine_kernel(*refs, dilations):
    n = len(dilations)
    o_refs, l_refs, out_ref = refs[:n], refs[n:2 * n], refs[2 * n]
    scratch = list(refs[2 * n + 1:])
    tm, width = out_ref.shape

    views = []
    for g, dil in enumerate(dilations):
        if dil == 1:
            views.append((lambda sl, o=o_refs[g]: o[0, :, sl], lambda sl, l=l_refs[g]: l[0, :, sl]))
            continue
        o_scr, l_scr = scratch.pop(0), scratch.pop(0)
        rows = tm // dil
        for c in range(width // LANES):
            sl = slice(c * LANES, (c + 1) * LANES)
            for r in range(dil):
                o_scr[c, pl.ds(r, rows, stride=dil), :] = o_refs[g][r, :, sl]
                l_scr[c, pl.ds(r, rows, stride=dil), :] = l_refs[g][r, :, sl]
        views.append((lambda sl, s=o_scr: s[sl.start // LANES], lambda sl, s=l_scr: s[sl.start // LANES]))

    for c in range(width // LANES):
        sl = slice(c * LANES, (c + 1) * LANES)
        lses = [lv(sl) for _, lv in views]
        top = functools.reduce(jnp.maximum, lses)
        ws = [jnp.exp(l - top) for l in lses]
        inv = 1.0 / functools.reduce(jnp.add, ws)
        acc = (ws[0] * inv) * views[0][0](sl)
        for w, (ov, _) in zip(ws[1:], views[1:]):
            acc = acc + (w * inv) * ov(sl)
        out_ref[:, sl] = acc.astype(out_ref.dtype)


def combine_groups(outs, lses, dilations, *, tm=256):
    width = outs[0].shape[2]
    S = outs[0].shape[0] * outs[0].shape[1]
    tm = min(tm, S)
    assert S % tm == 0 and all(tm % (8 * d) == 0 for d in dilations)
    specs = [pl.BlockSpec((d, tm // d, width), lambda i: (0, i, 0)) for d in dilations]
    scratch = []
    for d in dilations:
        if d > 1:
            scratch += [pltpu.VMEM((width // LANES, tm, LANES), F32)] * 2
    return pl.pallas_call(
        functools.partial(_combine_kernel, dilations=tuple(dilations)),
        grid=(S // tm,),
        in_specs=specs + specs,
        out_specs=pl.BlockSpec((tm, width), lambda i: (i, 0)),
        out_shape=jax.ShapeDtypeStruct((S, width), BF16),
        scratch_shapes=scratch,
        compiler_params=_params("parallel"),
        name="combine_groups",
    )(*outs, *lses)


def kernel(x, positions, norm_mix, norm_mlp, norm_final, mlp_w_up, mlp_w_down, moba_w_qkv, moba_w_o,
           conv_w_in, conv_w, conv_w_out, dil_w_qkv, dil_w_o):
    B, S, D = x.shape
    assert B == 1 and D % HEAD_DIM == 0
    n_heads = D // HEAD_DIM
    depth = norm_mix.shape[0]
    h = x[0]
    tables = rope_tables(positions[0])
    mlp_w_up, mlp_w_down, moba_w_qkv, moba_w_o, conv_w_in, conv_w_out, dil_w_qkv, dil_w_o = (
        w.astype(BF16) for w in (mlp_w_up, mlp_w_down, moba_w_qkv, moba_w_o, conv_w_in, conv_w_out,
                                 dil_w_qkv, dil_w_o))
    tn = 1024
    for i in range(depth):
        kind, j = i % N_MIXERS, i // N_MIXERS
        if kind == 0:
            qkv = norm_matmul_rope(h, norm_mix[i], moba_w_qkv, j, tables, n_out=3 * D, tn=tn)
            h = matmul_residual(moba_attention(qkv[0], n_heads), moba_w_o, j, h)
        elif kind == 1:
            b, z = conv_in(h, norm_mix[i], conv_w_in, j)
            h = conv_out(b, z, conv_w[j], conv_w_out, j, h)
        else:
            outs, lses = [], []
            for g, (window, dilation) in enumerate(DIL_GROUPS):
                qkv = norm_matmul_rope(h, norm_mix[i], dil_w_qkv, j, tables, n_out=3 * D,
                                       col0=g * 3 * D // tn, dilation=dilation, tn=tn)
                o_g, lse_g = dilated_group_attention(qkv, window, dilation, n_heads)
                outs.append(o_g)
                lses.append(lse_g)
            o = combine_groups(outs, lses, [d for _, d in DIL_GROUPS])
            h = matmul_residual([o], dil_w_o, j, h)
        g_final = norm_final if i == depth - 1 else None
        h = mlp_residual(h, norm_mlp[i], mlp_w_up, mlp_w_down, i, g_final)
    return h[None]
```

```python
import functools
import math

import jax
import jax.numpy as jnp
from jax import lax
from jax.experimental import pallas as pl
from jax.experimental.pallas import tpu as pltpu

HEAD_DIM = 128
MOBA_BLOCK = 256
MOBA_TOP_K = 3
MOBA_CHUNK = 4
CONV_WIDTH = 3
DIL_GROUPS = ((128, 1), (512, 4), (2048, 16))
ROT_DIM = HEAD_DIM // 4
ROPE_THETA = 500000.0
RMS_EPS = 1e-5
NEG_INF = -1e30
N_MIXERS = 3

LANES = 128
VMEM_LIMIT = 56 * 1024 * 1024

F32 = jnp.float32
BF16 = jnp.bfloat16


def _params(*sem):
    return pltpu.CompilerParams(dimension_semantics=sem, vmem_limit_bytes=VMEM_LIMIT)


def _rmsnorm_rows(x, g):
    y = x * lax.rsqrt(jnp.mean(x * x, axis=-1, keepdims=True) + RMS_EPS)
    return y * g


def _dot(a, b):
    return jnp.dot(a, b, preferred_element_type=F32)


def _dot_nt(a, b):
    return lax.dot_general(a, b, (((1,), (1,)), ((), ())), preferred_element_type=F32)


def _dot_tn(a, b):
    return lax.dot_general(a, b, (((0,), (0,)), ((), ())), preferred_element_type=F32)


def _lane_groups(x):
    return [x[:, g * LANES:(g + 1) * LANES] for g in range(x.shape[1] // LANES)]


def _rope_table_kernel(pos_ref, invf_ref, cos_ref, sa_ref, sb_ref):
    ang = pos_ref[...] * invf_ref[...]
    lane = lax.broadcasted_iota(jnp.int32, ang.shape, 1)
    c, s = jnp.cos(ang), jnp.sin(ang)
    half = ROT_DIM // 2
    cos_ref[...] = jnp.where(lane < ROT_DIM, c, 1.0)
    sa_ref[...] = jnp.where(lane < half, -s, 0.0)
    sb_ref[...] = jnp.where((lane >= half) & (lane < ROT_DIM), s, 0.0)


def rope_tables(positions):
    S = positions.shape[0]
    tm = min(S, 1024)
    half = ROT_DIM // 2
    inv_freq = ROPE_THETA ** (-jnp.arange(0, ROT_DIM, 2, dtype=F32) / ROT_DIM)
    invf = jnp.concatenate([inv_freq, inv_freq, jnp.zeros((LANES - 2 * half,), F32)])[None, :]
    pos = positions.astype(F32)[:, None]
    tab = jax.ShapeDtypeStruct((S, LANES), F32)
    return pl.pallas_call(
        _rope_table_kernel,
        grid=(S // tm,),
        in_specs=[pl.BlockSpec((tm, 1), lambda i: (i, 0)),
                  pl.BlockSpec((1, LANES), lambda i: (0, 0))],
        out_specs=[pl.BlockSpec((tm, LANES), lambda i: (i, 0))] * 3,
        out_shape=[tab, tab, tab],
        compiler_params=_params("arbitrary"),
        name="rope_tables",
    )(pos, invf)


def _apply_rope(t, cos, sa, sb):
    half = ROT_DIM // 2
    return (t * cos + pltpu.roll(t, LANES - half, 1) * sa + pltpu.roll(t, half, 1) * sb)


def _norm_mm_rope_kernel(x_ref, g_ref, w_ref, cos_ref, sa_ref, sb_ref, o_ref, xn_ref, *scratch,
                         tn, d_model, dilation):
    j = pl.program_id(1)

    @pl.when(j == 0)
    def _():
        xn_ref[...] = _rmsnorm_rows(x_ref[...], g_ref[...]).astype(BF16)

    acc = _dot(xn_ref[...], w_ref[...])
    roped = ((j * tn) // d_model) % 3 != 2
    heads = tn // HEAD_DIM
    if dilation == 1:
        cos = jnp.where(roped, cos_ref[...], 1.0)
        sa = jnp.where(roped, sa_ref[...], 0.0)
        sb = jnp.where(roped, sb_ref[...], 0.0)
        for c in range(heads):
            sl = slice(c * HEAD_DIM, (c + 1) * HEAD_DIM)
            o_ref[0, :, sl] = _apply_rope(acc[:, sl], cos, sa, sb).astype(o_ref.dtype)
    else:
        acc_ref, = scratch
        for c in range(heads):
            acc_ref[c] = acc[:, c * HEAD_DIM:(c + 1) * HEAD_DIM]
        rows = acc.shape[0] // dilation
        for r in range(dilation):
            pick = pl.ds(r, rows, stride=dilation)
            cos = jnp.where(roped, cos_ref[pick, :], 1.0)
            sa = jnp.where(roped, sa_ref[pick, :], 0.0)
            sb = jnp.where(roped, sb_ref[pick, :], 0.0)
            for c in range(heads):
                sl = slice(c * HEAD_DIM, (c + 1) * HEAD_DIM)
                o_ref[r, :, sl] = _apply_rope(acc_ref[c, pick, :], cos, sa, sb).astype(o_ref.dtype)


def norm_matmul_rope(x, g, w_all, layer, tables, *, n_out, col0=0, dilation=1, tm=512, tn=2048):
    S, D = x.shape
    tm = min(tm, S)
    assert S % tm == 0 and n_out % tn == 0 and D % tn == 0 and tm % (16 * dilation) == 0
    rows = tm // dilation
    scratch = [pltpu.VMEM((tm, D), BF16)]
    if dilation > 1:
        scratch.append(pltpu.VMEM((tn // HEAD_DIM, tm, HEAD_DIM), F32))
    return pl.pallas_call(
        functools.partial(_norm_mm_rope_kernel, tn=tn, d_model=D, dilation=dilation),
        grid=(S // tm, n_out // tn),
        in_specs=[pl.BlockSpec((tm, D), lambda i, j: (i, 0)),
                  pl.BlockSpec((1, D), lambda i, j: (0, 0)),
                  pl.BlockSpec((None, D, tn), lambda i, j: (layer, 0, col0 + j))]
                 + [pl.BlockSpec((tm, LANES), lambda i, j: (i, 0))] * 3,
        out_specs=pl.BlockSpec((dilation, rows, tn), lambda i, j: (0, i, j)),
        out_shape=jax.ShapeDtypeStruct((dilation, S // dilation, n_out), BF16),
        scratch_shapes=scratch,
        compiler_params=_params("parallel", "arbitrary"),
        name="norm_matmul_rope",
    )(x, g.reshape(1, D), w_all, *tables)


def _mm_res_kernel(*refs, n_parts, tiles_per_part):
    a_refs, (w_ref, h_ref, o_ref) = refs[:n_parts], refs[n_parts:]
    part = pl.program_id(0) // tiles_per_part
    for p, a_ref in enumerate(a_refs):
        @pl.when(part == p)
        def _(a_ref=a_ref):
            o_ref[...] = h_ref[...] + _dot(a_ref[...], w_ref[...])


def matmul_residual(a_parts, w_all, layer, h, *, tm=1024, tn=1024):
    n_parts = len(a_parts)
    Sp, K = a_parts[0].shape
    S, N = h.shape
    tm, tn = min(tm, Sp), min(tn, N)
    assert Sp * n_parts == S and Sp % tm == 0 and N % tn == 0
    tiles = Sp // tm

    def a_spec(p):
        return pl.BlockSpec((tm, K), lambda i, j: (jnp.clip(i - p * tiles, 0, tiles - 1), 0))

    return pl.pallas_call(
        functools.partial(_mm_res_kernel, n_parts=n_parts, tiles_per_part=tiles),
        grid=(S // tm, N // tn),
        in_specs=[a_spec(p) for p in range(n_parts)]
                 + [pl.BlockSpec((None, K, tn), lambda i, j: (layer, 0, j)),
                    pl.BlockSpec((tm, tn), lambda i, j: (i, j))],
        out_specs=pl.BlockSpec((tm, tn), lambda i, j: (i, j)),
        out_shape=jax.ShapeDtypeStruct((S, N), F32),
        compiler_params=_params("parallel", "arbitrary"),
        name="matmul_residual",
    )(*a_parts, w_all, h)


def _mlp_kernel(h_ref, g_ref, wu_ref, wd_ref, gf_ref, o_ref, xn_ref, *, final_norm):
    f = pl.program_id(1)

    @pl.when(f == 0)
    def _():
        h = h_ref[...]
        xn_ref[...] = _rmsnorm_rows(h, g_ref[...]).astype(BF16)
        o_ref[...] = h

    hid = jnp.maximum(_dot(xn_ref[...], wu_ref[...]), 0.0)
    o_ref[...] += _dot((hid * hid).astype(BF16), wd_ref[...])

    if final_norm:
        @pl.when(f == pl.num_programs(1) - 1)
        def _():
            o_ref[...] = _rmsnorm_rows(o_ref[...], gf_ref[...])


def mlp_residual(h, g, w_up_all, w_down_all, layer, g_final=None, *, tm=512, tf=1024):
    S, D = h.shape
    Fd = w_up_all.shape[2]
    tm, tf = min(tm, S), min(tf, Fd)
    assert S % tm == 0 and Fd % tf == 0
    final_norm = g_final is not None
    gf = (g_final if final_norm else g).reshape(1, D)
    return pl.pallas_call(
        functools.partial(_mlp_kernel, final_norm=final_norm),
        grid=(S // tm, Fd // tf),
        in_specs=[pl.BlockSpec((tm, D), lambda i, f: (i, 0)),
                  pl.BlockSpec((1, D), lambda i, f: (0, 0)),
                  pl.BlockSpec((None, D, tf), lambda i, f: (layer, 0, f)),
                  pl.BlockSpec((None, tf, D), lambda i, f: (layer, f, 0)),
                  pl.BlockSpec((1, D), lambda i, f: (0, 0))],
        out_specs=pl.BlockSpec((tm, D), lambda i, f: (i, 0)),
        out_shape=jax.ShapeDtypeStruct((S, D), F32),
        scratch_shapes=[pltpu.VMEM((tm, D), BF16)],
        compiler_params=_params("parallel", "arbitrary"),
        name="mlp_residual",
    )(h, g.reshape(1, D), w_up_all, w_down_all, gf)


def _moba_gate_kernel(q_ref, k_ref, qaug_ref, kmean_ref, *, nb, tq):
    blk = MOBA_BLOCK
    i = pl.program_id(1)

    @pl.when(i == 0)
    def _():
        for n in range(nb):
            kn = k_ref[n * blk:(n + 1) * blk, :].astype(F32)
            kmean_ref[n:n + 1, :] = jnp.mean(kn, axis=0, keepdims=True)

    q = q_ref[...]
    km = kmean_ref[...]
    km_hi = km.astype(BF16)
    r1 = km - km_hi.astype(F32)
    km_mid = r1.astype(BF16)
    km_lo = (r1 - km_mid.astype(F32)).astype(BF16)
    gate = _dot_nt(km_hi, q) + _dot_nt(km_mid, q) + _dot_nt(km_lo, q)

    blk_id = lax.broadcasted_iota(jnp.int32, (nb, tq), 0)
    tok = lax.broadcasted_iota(jnp.int32, (nb, tq), 1)
    qb = i * (tq // blk) + lax.shift_right_logical(tok, jnp.int32(blk.bit_length() - 1))
    gate = jnp.where(blk_id < qb, gate, NEG_INF)
    chosen = jnp.zeros((nb, tq), jnp.bool_)
    for n in range(min(MOBA_TOP_K, nb)):
        best = jnp.max(gate, axis=0, keepdims=True)
        idx = jnp.min(jnp.where(gate == best, blk_id, nb), axis=0, keepdims=True)
        hit = blk_id == idx
        chosen = chosen | (hit & (n < qb))
        gate = jnp.where(hit, -jnp.inf, gate)

    eye = (lax.broadcasted_iota(jnp.int32, (nb, LANES), 0)
           == lax.broadcasted_iota(jnp.int32, (nb, LANES), 1))
    picked = _dot_tn(jnp.where(chosen, 1.0, 0.0).astype(BF16), jnp.where(eye, 1.0, 0.0).astype(BF16))
    lane = lax.broadcasted_iota(jnp.int32, (tq, LANES), 1)
    qaug_ref[:, 0:HEAD_DIM] = q
    qaug_ref[:, HEAD_DIM:2 * HEAD_DIM] = jnp.where((picked > 0.5) | (lane >= nb), 0.0, NEG_INF).astype(BF16)


def _moba_attn_kernel(qlo_ref, qhi_ref, k_ref, v_ref, olo_ref, ohi_ref, ka_ref, va_ref, s_ref, own_ref, m_ref,
                      acc_ref, *, chunk, nb):
    blk = MOBA_BLOCK
    span = chunk * blk
    last_chunk = nb // chunk - 1
    n_slots = nb // chunk + 1
    p = pl.program_id(1)

    @pl.when(jnp.logical_and(pl.program_id(0) == 0, p == 0))
    def _():
        lane = lax.broadcasted_iota(jnp.int32, (blk, LANES), 1)
        ones_col = jnp.where(lane == 0, 1.0, 0.0).astype(BF16)
        for n in range(nb):
            rows = slice(n * blk, (n + 1) * blk)
            ka_ref[rows, HEAD_DIM:2 * HEAD_DIM] = jnp.where(lane == n, 1.0, 0.0).astype(BF16)
            va_ref[rows, HEAD_DIM:2 * HEAD_DIM] = ones_col

    @pl.when(p == 0)
    def _():
        ka_ref[:, 0:HEAD_DIM] = k_ref[...]
        va_ref[:, 0:HEAD_DIM] = v_ref[...]

    qbs = (p, nb - 1 - p)
    q_refs, o_refs = (qlo_ref, qhi_ref), (olo_ref, ohi_ref)
    log2_scale = HEAD_DIM ** -0.5 * math.log2(math.e)
    n_lo = (qbs[0] + chunk - 1) // chunk
    n_hi = (qbs[1] + chunk - 1) // chunk

    def slot(c):
        is_hi = c < n_hi
        is_lo = jnp.logical_and(c >= n_hi, c < n_hi + n_lo)
        ck = jnp.where(is_hi, c, jnp.where(is_lo, c - n_hi, last_chunk))
        return is_hi, jnp.where(is_hi, 1, 0), pl.multiple_of(ck * span, span)

    row = lax.broadcasted_iota(jnp.int32, (blk, blk), 0)
    col = lax.broadcasted_iota(jnp.int32, (blk, blk), 1)
    starts = [pl.multiple_of(qb * blk, blk) for qb in qbs]
    for t in range(2):
        s = _dot_nt(q_refs[t][:, 0:HEAD_DIM], ka_ref[pl.ds(starts[t], blk), 0:HEAD_DIM]) * log2_scale
        s = jnp.where(col <= row, s, NEG_INF)
        own_ref[t] = s
        m_ref[t] = functools.reduce(jnp.maximum, _lane_groups(s))
    for c in range(n_slots):
        is_hi, which, st = slot(c)
        qa = jnp.where(is_hi, qhi_ref[...], qlo_ref[...])
        s = _dot_nt(qa, ka_ref[pl.ds(st, span), :]) * log2_scale
        s_ref[c] = s
        m_ref[which] = functools.reduce(jnp.maximum, _lane_groups(s), m_ref[which])
    for t in range(2):
        m_ref[t] = jnp.broadcast_to(jnp.max(m_ref[t], axis=1, keepdims=True), (blk, LANES))

    for t in range(2):
        m = m_ref[t]
        pr = [jnp.exp2(s - m) for s in _lane_groups(own_ref[t])]
        acc_ref[t] = _dot(jnp.concatenate(pr, axis=1).astype(BF16), va_ref[pl.ds(starts[t], blk), :])
    for c in range(n_slots):
        _, which, st = slot(c)
        m = m_ref[which]
        pr = [jnp.exp2(s - m) for s in _lane_groups(s_ref[c])]
        acc_ref[which] += _dot(jnp.concatenate(pr, axis=1).astype(BF16), va_ref[pl.ds(st, span), :])
    for t in range(2):
        acc = acc_ref[t]
        o_refs[t][...] = (acc[:, 0:HEAD_DIM] * (1.0 / acc[:, HEAD_DIM:HEAD_DIM + 1])).astype(o_refs[t].dtype)


def moba_attention(qkv, n_heads, *, tq=1024):
    S = qkv.shape[0]
    blk, hd, H, chunk = MOBA_BLOCK, HEAD_DIM, n_heads, MOBA_CHUNK
    tq = min(tq, S)
    assert S % tq == 0 and tq % blk == 0 and hd == LANES and blk & (blk - 1) == 0
    nb = S // blk
    assert nb <= LANES and nb % 8 == 0 and nb % (2 * chunk) == 0
    half = nb // 2
    qaug = pl.pallas_call(
        functools.partial(_moba_gate_kernel, nb=nb, tq=tq),
        grid=(H, S // tq),
        in_specs=[pl.BlockSpec((tq, hd), lambda h, i: (i, h)),
                  pl.BlockSpec((S, hd), lambda h, i: (0, H + h))],
        out_specs=pl.BlockSpec((tq, 2 * hd), lambda h, i: (i, h)),
        out_shape=jax.ShapeDtypeStruct((S, H * 2 * hd), BF16),
        scratch_shapes=[pltpu.VMEM((nb, hd), F32)],
        compiler_params=_params("parallel", "arbitrary"),
        name="moba_gate",
    )(qkv, qkv)
    out = jax.ShapeDtypeStruct((S // 2, H * hd), BF16)
    return pl.pallas_call(
        functools.partial(_moba_attn_kernel, chunk=chunk, nb=nb),
        grid=(H, half),
        in_specs=[pl.BlockSpec((blk, 2 * hd), lambda h, p: (p, h)),
                  pl.BlockSpec((blk, 2 * hd), lambda h, p: (nb - 1 - p, h)),
                  pl.BlockSpec((S, hd), lambda h, p: (0, H + h)),
                  pl.BlockSpec((S, hd), lambda h, p: (0, 2 * H + h))],
        out_specs=[pl.BlockSpec((blk, hd), lambda h, p: (p, h)),
                   pl.BlockSpec((blk, hd), lambda h, p: (half - 1 - p, h))],
        out_shape=[out, out],
        scratch_shapes=[pltpu.VMEM((S, 2 * hd), BF16),
                        pltpu.VMEM((S, 2 * hd), BF16),
                        pltpu.VMEM((nb // chunk + 1, blk, chunk * blk), F32),
                        pltpu.VMEM((2, blk, blk), F32),
                        pltpu.VMEM((2, blk, LANES), F32),
                        pltpu.VMEM((2, blk, 2 * hd), F32)],
        compiler_params=_params("arbitrary", "arbitrary"),
        name="moba_attention",
    )(qaug, qaug, qkv, qkv)


TAIL = 8


def _conv_mix_kernel(x_ref, g_ref, wb_ref, wc_ref, wu_ref, cw_ref, a_ref, xn_ref, tail_ref):
    i, j = pl.program_id(0), pl.program_id(1)

    @pl.when(j == 0)
    def _():
        xn_ref[...] = _rmsnorm_rows(x_ref[...], g_ref[...]).astype(BF16)

    xn = xn_ref[...]
    z = _dot(xn, wc_ref[...]) * _dot(xn, wu_ref[...])
    tm = z.shape[0]
    prev = jnp.where(i == 0, 0.0, tail_ref[j])
    row = lax.broadcasted_iota(jnp.int32, z.shape, 0)
    zm1 = jnp.where(row == 0, prev[TAIL - 1:TAIL, :], pltpu.roll(z, 1, 0))
    zm2 = jnp.where(row == 0, prev[TAIL - 2:TAIL - 1, :],
                    jnp.where(row == 1, prev[TAIL - 1:TAIL, :], pltpu.roll(z, 2, 0)))
    cw = cw_ref[...]
    conv = cw[0:1, :] * zm2 + cw[1:2, :] * zm1 + cw[2:3, :] * z
    a_ref[...] = (_dot(xn, wb_ref[...]) * conv).astype(a_ref.dtype)
    tail_ref[j] = z[tm - TAIL:tm, :]


def conv_mix(x, g, w_in_all, layer, conv_w, *, tm=512, tn=512):
    S, D = x.shape
    tm, tn = min(tm, S), min(tn, D)
    assert S % tm == 0 and D % tn == 0 and w_in_all.shape[2] == 3 * D and tm % TAIL == 0
    assert conv_w.shape == (CONV_WIDTH, D)
    nj = D // tn
    return pl.pallas_call(
        _conv_mix_kernel,
        grid=(S // tm, nj),
        in_specs=[pl.BlockSpec((tm, D), lambda i, j: (i, 0)),
                  pl.BlockSpec((1, D), lambda i, j: (0, 0)),
                  pl.BlockSpec((None, D, tn), lambda i, j: (layer, 0, j)),
                  pl.BlockSpec((None, D, tn), lambda i, j: (layer, 0, nj + j)),
                  pl.BlockSpec((None, D, tn), lambda i, j: (layer, 0, 2 * nj + j)),
                  pl.BlockSpec((CONV_WIDTH, tn), lambda i, j: (0, j))],
        out_specs=pl.BlockSpec((tm, tn), lambda i, j: (i, j)),
        out_shape=jax.ShapeDtypeStruct((S, D), BF16),
        scratch_shapes=[pltpu.VMEM((tm, D), BF16), pltpu.VMEM((nj, TAIL, tn), F32)],
        compiler_params=_params("arbitrary", "arbitrary"),
        name="conv_mix",
    )(x, g.reshape(1, D), w_in_all, w_in_all, w_in_all, conv_w)


def _dilated_kernel(q_ref, k_ref, v_ref, o_ref, lse_ref, kband_ref, vband_ref, *, band, n_heads):
    j = pl.program_id(1)
    scale = HEAD_DIM ** -0.5

    @pl.when(j == 0)
    def _():
        kband_ref[0:band, :] = jnp.zeros((band, kband_ref.shape[1]), BF16)
        vband_ref[0:band, :] = jnp.zeros((band, vband_ref.shape[1]), BF16)

    kband_ref[band:2 * band, :] = k_ref[...]
    vband_ref[band:2 * band, :] = v_ref[...]
    a = lax.broadcasted_iota(jnp.int32, (band, 2 * band), 0)
    b = lax.broadcasted_iota(jnp.int32, (band, 2 * band), 1)
    valid = (b >= a) & (b <= a + band) & ((j > 0) | (b >= band))
    for hh in range(n_heads):
        sl = slice(hh * HEAD_DIM, (hh + 1) * HEAD_DIM)
        s = _dot_nt(q_ref[:, sl], kband_ref[:, sl]) * scale
        s = jnp.where(valid, s, NEG_INF)
        m = jnp.max(s, axis=1, keepdims=True)
        e = jnp.exp(s - m)
        l = jnp.sum(e, axis=1, keepdims=True)
        o_ref[:, sl] = (_dot(e.astype(BF16), vband_ref[:, sl]) * (1.0 / l)).astype(o_ref.dtype)
        lse_ref[:, sl] = jnp.broadcast_to(m + jnp.log(l), (band, HEAD_DIM))
    kband_ref[0:band, :] = k_ref[...]
    vband_ref[0:band, :] = v_ref[...]


def dilated_group_attention(qkv, window, dilation, n_heads):
    dil, L, W = qkv.shape
    band = window // dilation
    width = n_heads * HEAD_DIM
    assert dil == dilation and L % band == 0 and W == 3 * width
    nbl = L // band

    def spec(t):
        return pl.BlockSpec((None, band, width), lambda r, j: (r, j, t))

    return pl.pallas_call(
        functools.partial(_dilated_kernel, band=band, n_heads=n_heads),
        grid=(dil, nbl),
        in_specs=[spec(0), spec(1), spec(2)],
        out_specs=[pl.BlockSpec((None, band, width), lambda r, j: (r, j, 0))] * 2,
        out_shape=[jax.ShapeDtypeStruct((dil, L, width), BF16),
                   jax.ShapeDtypeStruct((dil, L, width), F32)],
        scratch_shapes=[pltpu.VMEM((2 * band, width), BF16)] * 2,
        compiler_params=_params("parallel", "arbitrary"),
        name="dilated_attention",
    )(qkv, qkv, qkv)


def _combine_kernel(*refs, dilations):
    n = len(dilations)
    o_refs, l_refs, out_ref = refs[:n], refs[n:2 * n], refs[2 * n]
    scratch = list(refs[2 * n + 1:])
    tm, width = out_ref.shape

    views = []
    for g, dil in enumerate(dilations):
        if dil == 1:
            views.append((lambda sl, o=o_refs[g]: o[0, :, sl].astype(F32), lambda sl, l=l_refs[g]: l[0, :, sl]))
            continue
        o_scr, l_scr = scratch.pop(0), scratch.pop(0)
        rows = tm // dil
        for c in range(width // LANES):
            sl = slice(c * LANES, (c + 1) * LANES)
            for r in range(dil):
                o_scr[c, pl.ds(r, rows, stride=dil), :] = o_refs[g][r, :, sl].astype(F32)
                l_scr[c, pl.ds(r, rows, stride=dil), :] = l_refs[g][r, :, sl]
        views.append((lambda sl, s=o_scr: s[sl.start // LANES], lambda sl, s=l_scr: s[sl.start // LANES]))

    for c in range(width // LANES):
        sl = slice(c * LANES, (c + 1) * LANES)
        lses = [lv(sl) for _, lv in views]
        top = functools.reduce(jnp.maximum, lses)
        ws = [jnp.exp(l - top) for l in lses]
        inv = 1.0 / functools.reduce(jnp.add, ws)
        acc = (ws[0] * inv) * views[0][0](sl)
        for w, (ov, _) in zip(ws[1:], views[1:]):
            acc = acc + (w * inv) * ov(sl)
        out_ref[:, sl] = acc.astype(out_ref.dtype)


def combine_groups(outs, lses, dilations, *, tm=256):
    width = outs[0].shape[2]
    S = outs[0].shape[0] * outs[0].shape[1]
    tm = min(tm, S)
    assert S % tm == 0 and all(tm % (8 * d) == 0 for d in dilations)
    specs = [pl.BlockSpec((d, tm // d, width), lambda i: (0, i, 0)) for d in dilations]
    scratch = []
    for d in dilations:
        if d > 1:
            scratch += [pltpu.VMEM((width // LANES, tm, LANES), F32)] * 2
    return pl.pallas_call(
        functools.partial(_combine_kernel, dilations=tuple(dilations)),
        grid=(S // tm,),
        in_specs=specs + specs,
        out_specs=pl.BlockSpec((tm, width), lambda i: (i, 0)),
        out_shape=jax.ShapeDtypeStruct((S, width), BF16),
        scratch_shapes=scratch,
        compiler_params=_params("parallel"),
        name="combine_groups",
    )(*outs, *lses)


def kernel(x, positions, norm_mix, norm_mlp, norm_final, mlp_w_up, mlp_w_down, moba_w_qkv, moba_w_o,
           conv_w_in, conv_w, conv_w_out, dil_w_qkv, dil_w_o):
    B, S, D = x.shape
    assert B == 1 and D % HEAD_DIM == 0
    n_heads = D // HEAD_DIM
    depth = norm_mix.shape[0]
    h = x[0]
    tables = rope_tables(positions[0])
    mlp_w_up, mlp_w_down, moba_w_qkv, moba_w_o, conv_w_in, conv_w_out, dil_w_qkv, dil_w_o = (
        w.astype(BF16) for w in (mlp_w_up, mlp_w_down, moba_w_qkv, moba_w_o, conv_w_in, conv_w_out,
                                 dil_w_qkv, dil_w_o))
    tn = D
    for i in range(depth):
        kind, j = i % N_MIXERS, i // N_MIXERS
        if kind == 0:
            qkv = norm_matmul_rope(h, norm_mix[i], moba_w_qkv, j, tables, n_out=3 * D, tn=tn)
            h = matmul_residual(moba_attention(qkv[0], n_heads), moba_w_o, j, h)
        elif kind == 1:
            a = conv_mix(h, norm_mix[i], conv_w_in, j, conv_w[j])
            h = matmul_residual([a], conv_w_out, j, h)
        else:
            outs, lses = [], []
            for g, (window, dilation) in enumerate(DIL_GROUPS):
                qkv = norm_matmul_rope(h, norm_mix[i], dil_w_qkv, j, tables, n_out=3 * D,
                                       col0=g * 3 * D // tn, dilation=dilation, tn=tn)
                o_g, lse_g = dilated_group_attention(qkv, window, dilation, n_heads)
                outs.append(o_g)
                lses.append(lse_g)
            o = combine_groups(outs, lses, [d for _, d in DIL_GROUPS])
            h = matmul_residual([o], dil_w_o, j, h)
        g_final = norm_final if i == depth - 1 else None
        h = mlp_residual(h, norm_mlp[i], mlp_w_up, mlp_w_down, i, g_final)
    return h[None]
```

```python
import functools
import math

import jax
import jax.numpy as jnp
from jax import lax
from jax.experimental import pallas as pl
from jax.experimental.pallas import tpu as pltpu

HEAD_DIM = 128
MOBA_BLOCK = 256
MOBA_TOP_K = 3
MOBA_CHUNK = 4
CONV_WIDTH = 3
DIL_GROUPS = ((128, 1), (512, 4), (2048, 16))
ROT_DIM = HEAD_DIM // 4
ROPE_THETA = 500000.0
RMS_EPS = 1e-5
NEG_INF = -1e30
N_MIXERS = 3

LANES = 128
VMEM_LIMIT = 56 * 1024 * 1024
MOBA_VMEM_LIMIT = 60 * 1024 * 1024

F32 = jnp.float32
BF16 = jnp.bfloat16


def _params(*sem):
    return pltpu.CompilerParams(dimension_semantics=sem, vmem_limit_bytes=VMEM_LIMIT)


def _rmsnorm_rows(x, g):
    y = x * lax.rsqrt(jnp.mean(x * x, axis=-1, keepdims=True) + RMS_EPS)
    return y * g


def _dot(a, b):
    return jnp.dot(a, b, preferred_element_type=F32)


def _dot_nt(a, b):
    return lax.dot_general(a, b, (((1,), (1,)), ((), ())), preferred_element_type=F32)


def _dot_tn(a, b):
    return lax.dot_general(a, b, (((0,), (0,)), ((), ())), preferred_element_type=F32)


def _lane_groups(x):
    return [x[:, g * LANES:(g + 1) * LANES] for g in range(x.shape[1] // LANES)]


def _rope_table_kernel(pos_ref, invf_ref, cos_ref, sa_ref, sb_ref):
    ang = pos_ref[...] * invf_ref[...]
    lane = lax.broadcasted_iota(jnp.int32, ang.shape, 1)
    c, s = jnp.cos(ang), jnp.sin(ang)
    half = ROT_DIM // 2
    cos_ref[...] = jnp.where(lane < ROT_DIM, c, 1.0)
    sa_ref[...] = jnp.where(lane < half, -s, 0.0)
    sb_ref[...] = jnp.where((lane >= half) & (lane < ROT_DIM), s, 0.0)


def rope_tables(positions):
    S = positions.shape[0]
    tm = min(S, 1024)
    half = ROT_DIM // 2
    inv_freq = ROPE_THETA ** (-jnp.arange(0, ROT_DIM, 2, dtype=F32) / ROT_DIM)
    invf = jnp.concatenate([inv_freq, inv_freq, jnp.zeros((LANES - 2 * half,), F32)])[None, :]
    pos = positions.astype(F32)[:, None]
    tab = jax.ShapeDtypeStruct((S, LANES), F32)
    return pl.pallas_call(
        _rope_table_kernel,
        grid=(S // tm,),
        in_specs=[pl.BlockSpec((tm, 1), lambda i: (i, 0)),
                  pl.BlockSpec((1, LANES), lambda i: (0, 0))],
        out_specs=[pl.BlockSpec((tm, LANES), lambda i: (i, 0))] * 3,
        out_shape=[tab, tab, tab],
        compiler_params=_params("arbitrary"),
        name="rope_tables",
    )(pos, invf)


def _apply_rope(t, cos, sa, sb):
    half = ROT_DIM // 2
    return (t * cos + pltpu.roll(t, LANES - half, 1) * sa + pltpu.roll(t, half, 1) * sb)


def _norm_mm_rope_kernel(x_ref, g_ref, w_ref, cos_ref, sa_ref, sb_ref, o_ref, xn_ref, *scratch,
                         tn, d_model, dilation):
    j = pl.program_id(1)

    @pl.when(j == 0)
    def _():
        xn_ref[...] = _rmsnorm_rows(x_ref[...], g_ref[...]).astype(BF16)

    acc = _dot(xn_ref[...], w_ref[...])
    roped = ((j * tn) // d_model) % 3 != 2
    heads = tn // HEAD_DIM
    if dilation == 1:
        cos = jnp.where(roped, cos_ref[...], 1.0)
        sa = jnp.where(roped, sa_ref[...], 0.0)
        sb = jnp.where(roped, sb_ref[...], 0.0)
        for c in range(heads):
            sl = slice(c * HEAD_DIM, (c + 1) * HEAD_DIM)
            o_ref[0, :, sl] = _apply_rope(acc[:, sl], cos, sa, sb).astype(o_ref.dtype)
    else:
        acc_ref, = scratch
        for c in range(heads):
            acc_ref[c] = acc[:, c * HEAD_DIM:(c + 1) * HEAD_DIM]
        rows = acc.shape[0] // dilation
        for r in range(dilation):
            pick = pl.ds(r, rows, stride=dilation)
            cos = jnp.where(roped, cos_ref[pick, :], 1.0)
            sa = jnp.where(roped, sa_ref[pick, :], 0.0)
            sb = jnp.where(roped, sb_ref[pick, :], 0.0)
            for c in range(heads):
                sl = slice(c * HEAD_DIM, (c + 1) * HEAD_DIM)
                o_ref[r, :, sl] = _apply_rope(acc_ref[c, pick, :], cos, sa, sb).astype(o_ref.dtype)


def norm_matmul_rope(x, g, w_all, layer, tables, *, n_out, col0=0, dilation=1, tm=512, tn=2048):
    S, D = x.shape
    tm = min(tm, S)
    assert S % tm == 0 and n_out % tn == 0 and D % tn == 0 and tm % (16 * dilation) == 0
    rows = tm // dilation
    scratch = [pltpu.VMEM((tm, D), BF16)]
    if dilation > 1:
        scratch.append(pltpu.VMEM((tn // HEAD_DIM, tm, HEAD_DIM), F32))
    return pl.pallas_call(
        functools.partial(_norm_mm_rope_kernel, tn=tn, d_model=D, dilation=dilation),
        grid=(S // tm, n_out // tn),
        in_specs=[pl.BlockSpec((tm, D), lambda i, j: (i, 0)),
                  pl.BlockSpec((1, D), lambda i, j: (0, 0)),
                  pl.BlockSpec((None, D, tn), lambda i, j: (layer, 0, col0 + j))]
                 + [pl.BlockSpec((tm, LANES), lambda i, j: (i, 0))] * 3,
        out_specs=pl.BlockSpec((dilation, rows, tn), lambda i, j: (0, i, j)),
        out_shape=jax.ShapeDtypeStruct((dilation, S // dilation, n_out), BF16),
        scratch_shapes=scratch,
        compiler_params=_params("parallel", "arbitrary"),
        name="norm_matmul_rope",
    )(x, g.reshape(1, D), w_all, *tables)


def _mm_res_kernel(*refs, n_parts, tiles_per_part):
    a_refs, (w_ref, h_ref, o_ref) = refs[:n_parts], refs[n_parts:]
    part = pl.program_id(0) // tiles_per_part
    for p, a_ref in enumerate(a_refs):
        @pl.when(part == p)
        def _(a_ref=a_ref):
            o_ref[...] = h_ref[...] + _dot(a_ref[...], w_ref[...])


def matmul_residual(a_parts, w_all, layer, h, *, tm=1024, tn=1024):
    n_parts = len(a_parts)
    Sp, K = a_parts[0].shape
    S, N = h.shape
    tm, tn = min(tm, Sp), min(tn, N)
    assert Sp * n_parts == S and Sp % tm == 0 and N % tn == 0
    tiles = Sp // tm

    def a_spec(p):
        return pl.BlockSpec((tm, K), lambda i, j: (jnp.clip(i - p * tiles, 0, tiles - 1), 0))

    return pl.pallas_call(
        functools.partial(_mm_res_kernel, n_parts=n_parts, tiles_per_part=tiles),
        grid=(S // tm, N // tn),
        in_specs=[a_spec(p) for p in range(n_parts)]
                 + [pl.BlockSpec((None, K, tn), lambda i, j: (layer, 0, j)),
                    pl.BlockSpec((tm, tn), lambda i, j: (i, j))],
        out_specs=pl.BlockSpec((tm, tn), lambda i, j: (i, j)),
        out_shape=jax.ShapeDtypeStruct((S, N), F32),
        compiler_params=_params("parallel", "arbitrary"),
        name="matmul_residual",
    )(*a_parts, w_all, h)


def _mlp_kernel(*refs, final_norm, n_cast):
    h_ref, g_ref, wu_ref, wd_ref, gf_ref = refs[:5]
    src_refs = refs[5:5 + n_cast]
    o_ref = refs[5 + n_cast]
    dst_refs = refs[6 + n_cast:6 + 2 * n_cast]
    xn_ref = refs[6 + 2 * n_cast]
    f = pl.program_id(1)

    @pl.when(f == 0)
    def _():
        h = h_ref[...]
        xn_ref[...] = _rmsnorm_rows(h, g_ref[...]).astype(BF16)
        o_ref[...] = h

    hid = jnp.maximum(_dot(xn_ref[...], wu_ref[...]), 0.0)
    o_ref[...] += _dot((hid * hid).astype(BF16), wd_ref[...])

    if final_norm:
        @pl.when(f == pl.num_programs(1) - 1)
        def _():
            o_ref[...] = _rmsnorm_rows(o_ref[...], gf_ref[...])

    for src_ref, dst_ref in zip(src_refs, dst_refs):
        dst_ref[...] = src_ref[...].astype(dst_ref.dtype)


def mlp_residual(h, g, w_up_all, w_down_all, layer, g_final=None, cast_next=(), *, tm=512, tf=1024):
    S, D = h.shape
    Fd = w_up_all.shape[2]
    tm, tf = min(tm, S), min(tf, Fd)
    assert S % tm == 0 and Fd % tf == 0
    nf = Fd // tf
    n_steps = (S // tm) * nf
    final_norm = g_final is not None
    gf = (g_final if final_norm else g).reshape(1, D)
    cast_in, cast_out, cast_shapes = [], [], []
    for w_all, idx in cast_next:
        _, K, N = w_all.shape
        rows = K // n_steps
        assert rows * n_steps == K and rows % 16 == 0
        cast_in.append(pl.BlockSpec((None, rows, N), lambda i, f, idx=idx: (idx, i * nf + f, 0)))
        cast_out.append(pl.BlockSpec((None, rows, N), lambda i, f: (0, i * nf + f, 0)))
        cast_shapes.append(jax.ShapeDtypeStruct((1, K, N), BF16))
    outs = pl.pallas_call(
        functools.partial(_mlp_kernel, final_norm=final_norm, n_cast=len(cast_next)),
        grid=(S // tm, nf),
        in_specs=[pl.BlockSpec((tm, D), lambda i, f: (i, 0)),
                  pl.BlockSpec((1, D), lambda i, f: (0, 0)),
                  pl.BlockSpec((None, D, tf), lambda i, f: (layer, 0, f)),
                  pl.BlockSpec((None, tf, D), lambda i, f: (layer, f, 0)),
                  pl.BlockSpec((1, D), lambda i, f: (0, 0))] + cast_in,
        out_specs=[pl.BlockSpec((tm, D), lambda i, f: (i, 0))] + cast_out,
        out_shape=[jax.ShapeDtypeStruct((S, D), F32)] + cast_shapes,
        scratch_shapes=[pltpu.VMEM((tm, D), BF16)],
        compiler_params=_params("arbitrary", "arbitrary"),
        name="mlp_residual",
    )(h, g.reshape(1, D), w_up_all, w_down_all, gf, *[w for w, _ in cast_next])
    return outs[0], list(outs[1:])


def _moba_gate_kernel(q_ref, k_ref, qaug_ref, kmean_ref, *, nb, tq):
    blk = MOBA_BLOCK
    i = pl.program_id(1)

    @pl.when(i == 0)
    def _():
        for n in range(nb):
            kn = k_ref[n * blk:(n + 1) * blk, :].astype(F32)
            kmean_ref[n:n + 1, :] = jnp.mean(kn, axis=0, keepdims=True)

    q = q_ref[...]
    km = kmean_ref[...]
    km_hi = km.astype(BF16)
    r1 = km - km_hi.astype(F32)
    km_mid = r1.astype(BF16)
    km_lo = (r1 - km_mid.astype(F32)).astype(BF16)
    gate = _dot_nt(km_hi, q) + _dot_nt(km_mid, q) + _dot_nt(km_lo, q)

    blk_id = lax.broadcasted_iota(jnp.int32, (nb, tq), 0)
    tok = lax.broadcasted_iota(jnp.int32, (nb, tq), 1)
    qb = i * (tq // blk) + lax.shift_right_logical(tok, jnp.int32(blk.bit_length() - 1))
    gate = jnp.where(blk_id < qb, gate, NEG_INF)
    chosen = jnp.zeros((nb, tq), jnp.bool_)
    for n in range(min(MOBA_TOP_K, nb)):
        best = jnp.max(gate, axis=0, keepdims=True)
        idx = jnp.min(jnp.where(gate == best, blk_id, nb), axis=0, keepdims=True)
        hit = blk_id == idx
        chosen = chosen | (hit & (n < qb))
        gate = jnp.where(hit, -jnp.inf, gate)

    eye = (lax.broadcasted_iota(jnp.int32, (nb, LANES), 0)
           == lax.broadcasted_iota(jnp.int32, (nb, LANES), 1))
    picked = _dot_tn(jnp.where(chosen, 1.0, 0.0).astype(BF16), jnp.where(eye, 1.0, 0.0).astype(BF16))
    lane = lax.broadcasted_iota(jnp.int32, (tq, LANES), 1)
    qaug_ref[:, 0:HEAD_DIM] = q
    qaug_ref[:, HEAD_DIM:2 * HEAD_DIM] = jnp.where((picked > 0.5) | (lane >= nb), 0.0, NEG_INF).astype(BF16)


def _moba_attn_kernel(qlo_ref, qhi_ref, k_ref, v_ref, olo_ref, ohi_ref, ka_ref, va_ref, s_ref, own_ref, m_ref,
                      acc_ref, *, chunk, nb, heads):
    blk = MOBA_BLOCK
    span = chunk * blk
    last_chunk = nb // chunk - 1
    n_slots = nb // chunk + 1
    p = pl.program_id(1)
    hd2 = 2 * HEAD_DIM

    @pl.when(jnp.logical_and(pl.program_id(0) == 0, p == 0))
    def _():
        lane = lax.broadcasted_iota(jnp.int32, (blk, LANES), 1)
        ones_col = jnp.where(lane == 0, 1.0, 0.0).astype(BF16)
        for n in range(nb):
            rows = slice(n * blk, (n + 1) * blk)
            for g in range(heads):
                ka_ref[g, rows, HEAD_DIM:hd2] = jnp.where(lane == n, 1.0, 0.0).astype(BF16)
                va_ref[g, rows, HEAD_DIM:hd2] = ones_col

    @pl.when(p == 0)
    def _():
        for g in range(heads):
            ka_ref[g, :, 0:HEAD_DIM] = k_ref[:, g * HEAD_DIM:(g + 1) * HEAD_DIM]
            va_ref[g, :, 0:HEAD_DIM] = v_ref[:, g * HEAD_DIM:(g + 1) * HEAD_DIM]

    qbs = (p, nb - 1 - p)
    q_refs, o_refs = (qlo_ref, qhi_ref), (olo_ref, ohi_ref)
    log2_scale = HEAD_DIM ** -0.5 * math.log2(math.e)
    n_lo = (qbs[0] + chunk - 1) // chunk
    n_hi = (qbs[1] + chunk - 1) // chunk

    def slot(c):
        is_hi = c < n_hi
        is_lo = jnp.logical_and(c >= n_hi, c < n_hi + n_lo)
        ck = jnp.where(is_hi, c, jnp.where(is_lo, c - n_hi, last_chunk))
        return is_hi, jnp.where(is_hi, 1, 0), pl.multiple_of(ck * span, span)

    row = lax.broadcasted_iota(jnp.int32, (blk, blk), 0)
    col = lax.broadcasted_iota(jnp.int32, (blk, blk), 1)
    starts = [pl.multiple_of(qb * blk, blk) for qb in qbs]
    for t in range(2):
        for g in range(heads):
            q = q_refs[t][:, g * hd2:g * hd2 + HEAD_DIM]
            s = _dot_nt(q, ka_ref[g, pl.ds(starts[t], blk), 0:HEAD_DIM]) * log2_scale
            s = jnp.where(col <= row, s, NEG_INF)
            own_ref[g, t] = s
            m_ref[g, t] = functools.reduce(jnp.maximum, _lane_groups(s))
    for c in range(n_slots):
        is_hi, which, st = slot(c)
        for g in range(heads):
            cols = slice(g * hd2, (g + 1) * hd2)
            qa = jnp.where(is_hi, qhi_ref[:, cols], qlo_ref[:, cols])
            s = _dot_nt(qa, ka_ref[g, pl.ds(st, span), :]) * log2_scale
            s_ref[g, c] = s
            m_ref[g, which] = functools.reduce(jnp.maximum, _lane_groups(s), m_ref[g, which])
    for t in range(2):
        for g in range(heads):
            m_ref[g, t] = jnp.broadcast_to(jnp.max(m_ref[g, t], axis=1, keepdims=True), (blk, LANES))

    for t in range(2):
        for g in range(heads):
            m = m_ref[g, t]
            pr = [jnp.exp2(s - m) for s in _lane_groups(own_ref[g, t])]
            acc_ref[g, t] = _dot(jnp.concatenate(pr, axis=1).astype(BF16), va_ref[g, pl.ds(starts[t], blk), :])
    for c in range(n_slots):
        _, which, st = slot(c)
        for g in range(heads):
            m = m_ref[g, which]
            pr = [jnp.exp2(s - m) for s in _lane_groups(s_ref[g, c])]
            acc_ref[g, which] += _dot(jnp.concatenate(pr, axis=1).astype(BF16), va_ref[g, pl.ds(st, span), :])
    for t in range(2):
        for g in range(heads):
            acc = acc_ref[g, t]
            out = acc[:, 0:HEAD_DIM] * (1.0 / acc[:, HEAD_DIM:HEAD_DIM + 1])
            o_refs[t][:, g * HEAD_DIM:(g + 1) * HEAD_DIM] = out.astype(o_refs[t].dtype)


def moba_attention(qkv, n_heads, *, tq=1024, heads=2):
    S = qkv.shape[0]
    blk, hd, H, chunk = MOBA_BLOCK, HEAD_DIM, n_heads, MOBA_CHUNK
    tq = min(tq, S)
    assert S % tq == 0 and tq % blk == 0 and hd == LANES and blk & (blk - 1) == 0
    nb = S // blk
    assert nb <= LANES and nb % 8 == 0 and nb % (2 * chunk) == 0
    half = nb // 2
    qaug = pl.pallas_call(
        functools.partial(_moba_gate_kernel, nb=nb, tq=tq),
        grid=(H, S // tq),
        in_specs=[pl.BlockSpec((tq, hd), lambda h, i: (i, h)),
                  pl.BlockSpec((S, hd), lambda h, i: (0, H + h))],
        out_specs=pl.BlockSpec((tq, 2 * hd), lambda h, i: (i, h)),
        out_shape=jax.ShapeDtypeStruct((S, H * 2 * hd), BF16),
        scratch_shapes=[pltpu.VMEM((nb, hd), F32)],
        compiler_params=_params("parallel", "arbitrary"),
        name="moba_gate",
    )(qkv, qkv)
    out = jax.ShapeDtypeStruct((S // 2, H * hd), BF16)
    G = heads
    assert H % G == 0
    return pl.pallas_call(
        functools.partial(_moba_attn_kernel, chunk=chunk, nb=nb, heads=G),
        grid=(H // G, half),
        in_specs=[pl.BlockSpec((blk, G * 2 * hd), lambda h, p: (p, h)),
                  pl.BlockSpec((blk, G * 2 * hd), lambda h, p: (nb - 1 - p, h)),
                  pl.BlockSpec((S, G * hd), lambda h, p: (0, H // G + h)),
                  pl.BlockSpec((S, G * hd), lambda h, p: (0, 2 * (H // G) + h))],
        out_specs=[pl.BlockSpec((blk, G * hd), lambda h, p: (p, h)),
                   pl.BlockSpec((blk, G * hd), lambda h, p: (half - 1 - p, h))],
        out_shape=[out, out],
        scratch_shapes=[pltpu.VMEM((G, S, 2 * hd), BF16),
                        pltpu.VMEM((G, S, 2 * hd), BF16),
                        pltpu.VMEM((G, nb // chunk + 1, blk, chunk * blk), F32),
                        pltpu.VMEM((G, 2, blk, blk), F32),
                        pltpu.VMEM((G, 2, blk, LANES), F32),
                        pltpu.VMEM((G, 2, blk, 2 * hd), F32)],
        compiler_params=pltpu.CompilerParams(dimension_semantics=("arbitrary", "arbitrary"),
                                             vmem_limit_bytes=MOBA_VMEM_LIMIT),
        name="moba_attention",
    )(qaug, qaug, qkv, qkv)


TAIL = 8


def _conv_mix_kernel(x_ref, g_ref, wb_ref, wc_ref, wu_ref, cw_ref, a_ref, xn_ref, tail_ref):
    i, j = pl.program_id(0), pl.program_id(1)

    @pl.when(j == 0)
    def _():
        xn_ref[...] = _rmsnorm_rows(x_ref[...], g_ref[...]).astype(BF16)

    xn = xn_ref[...]
    z = _dot(xn, wc_ref[...]) * _dot(xn, wu_ref[...])
    tm = z.shape[0]
    prev = jnp.where(i == 0, 0.0, tail_ref[j])
    row = lax.broadcasted_iota(jnp.int32, z.shape, 0)
    zm1 = jnp.where(row == 0, prev[TAIL - 1:TAIL, :], pltpu.roll(z, 1, 0))
    zm2 = jnp.where(row == 0, prev[TAIL - 2:TAIL - 1, :],
                    jnp.where(row == 1, prev[TAIL - 1:TAIL, :], pltpu.roll(z, 2, 0)))
    cw = cw_ref[...]
    conv = cw[0:1, :] * zm2 + cw[1:2, :] * zm1 + cw[2:3, :] * z
    a_ref[...] = (_dot(xn, wb_ref[...]) * conv).astype(a_ref.dtype)
    tail_ref[j] = z[tm - TAIL:tm, :]


def conv_mix(x, g, w_in_all, layer, conv_w, *, tm=512, tn=512):
    S, D = x.shape
    tm, tn = min(tm, S), min(tn, D)
    assert S % tm == 0 and D % tn == 0 and w_in_all.shape[2] == 3 * D and tm % TAIL == 0
    assert conv_w.shape == (CONV_WIDTH, D)
    nj = D // tn
    return pl.pallas_call(
        _conv_mix_kernel,
        grid=(S // tm, nj),
        in_specs=[pl.BlockSpec((tm, D), lambda i, j: (i, 0)),
                  pl.BlockSpec((1, D), lambda i, j: (0, 0)),
                  pl.BlockSpec((None, D, tn), lambda i, j: (layer, 0, j)),
                  pl.BlockSpec((None, D, tn), lambda i, j: (layer, 0, nj + j)),
                  pl.BlockSpec((None, D, tn), lambda i, j: (layer, 0, 2 * nj + j)),
                  pl.BlockSpec((CONV_WIDTH, tn), lambda i, j: (0, j))],
        out_specs=pl.BlockSpec((tm, tn), lambda i, j: (i, j)),
        out_shape=jax.ShapeDtypeStruct((S, D), BF16),
        scratch_shapes=[pltpu.VMEM((tm, D), BF16), pltpu.VMEM((nj, TAIL, tn), F32)],
        compiler_params=_params("arbitrary", "arbitrary"),
        name="conv_mix",
    )(x, g.reshape(1, D), w_in_all, w_in_all, w_in_all, conv_w)


def _dilated_kernel(q_ref, k_ref, v_ref, o_ref, lse_ref, kband_ref, vband_ref, *, band, n_heads):
    j = pl.program_id(1)
    scale = HEAD_DIM ** -0.5

    @pl.when(j == 0)
    def _():
        kband_ref[0:band, :] = jnp.zeros((band, kband_ref.shape[1]), BF16)
        vband_ref[0:band, :] = jnp.zeros((band, vband_ref.shape[1]), BF16)

    kband_ref[band:2 * band, :] = k_ref[...]
    vband_ref[band:2 * band, :] = v_ref[...]
    a = lax.broadcasted_iota(jnp.int32, (band, 2 * band), 0)
    b = lax.broadcasted_iota(jnp.int32, (band, 2 * band), 1)
    valid = (b >= a) & (b <= a + band) & ((j > 0) | (b >= band))
    for hh in range(n_heads):
        sl = slice(hh * HEAD_DIM, (hh + 1) * HEAD_DIM)
        s = _dot_nt(q_ref[:, sl], kband_ref[:, sl]) * scale
        s = jnp.where(valid, s, NEG_INF)
        m = jnp.max(s, axis=1, keepdims=True)
        e = jnp.exp(s - m)
        l = jnp.sum(e, axis=1, keepdims=True)
        o_ref[:, sl] = (_dot(e.astype(BF16), vband_ref[:, sl]) * (1.0 / l)).astype(o_ref.dtype)
        lse_ref[:, sl] = jnp.broadcast_to(m + jnp.log(l), (band, HEAD_DIM))
    kband_ref[0:band, :] = k_ref[...]
    vband_ref[0:band, :] = v_ref[...]


def dilated_group_attention(qkv, window, dilation, n_heads):
    dil, L, W = qkv.shape
    band = window // dilation
    width = n_heads * HEAD_DIM
    assert dil == dilation and L % band == 0 and W == 3 * width
    nbl = L // band

    def spec(t):
        return pl.BlockSpec((None, band, width), lambda r, j: (r, j, t))

    return pl.pallas_call(
        functools.partial(_dilated_kernel, band=band, n_heads=n_heads),
        grid=(dil, nbl),
        in_specs=[spec(0), spec(1), spec(2)],
        out_specs=[pl.BlockSpec((None, band, width), lambda r, j: (r, j, 0))] * 2,
        out_shape=[jax.ShapeDtypeStruct((dil, L, width), BF16),
                   jax.ShapeDtypeStruct((dil, L, width), F32)],
        scratch_shapes=[pltpu.VMEM((2 * band, width), BF16)] * 2,
        compiler_params=_params("parallel", "arbitrary"),
        name="dilated_attention",
    )(qkv, qkv, qkv)


def _combine_kernel(*refs, dilations):
    n = len(dilations)
    o_refs, l_refs, out_ref = refs[:n], refs[n:2 * n], refs[2 * n]
    scratch = list(refs[2 * n + 1:])
    tm, width = out_ref.shape

    views = []
    for g, dil in enumerate(dilations):
        if dil == 1:
            views.append((lambda sl, o=o_refs[g]: o[0, :, sl].astype(F32), lambda sl, l=l_refs[g]: l[0, :, sl]))
            continue
        o_scr, l_scr = scratch.pop(0), scratch.pop(0)
        rows = tm // dil
        for c in range(width // LANES):
            sl = slice(c * LANES, (c + 1) * LANES)
            for r in range(dil):
                o_scr[c, pl.ds(r, rows, stride=dil), :] = o_refs[g][r, :, sl].astype(F32)
                l_scr[c, pl.ds(r, rows, stride=dil), :] = l_refs[g][r, :, sl]
        views.append((lambda sl, s=o_scr: s[sl.start // LANES], lambda sl, s=l_scr: s[sl.start // LANES]))

    for c in range(width // LANES):
        sl = slice(c * LANES, (c + 1) * LANES)
        lses = [lv(sl) for _, lv in views]
        top = functools.reduce(jnp.maximum, lses)
        ws = [jnp.exp(l - top) for l in lses]
        inv = 1.0 / functools.reduce(jnp.add, ws)
        acc = (ws[0] * inv) * views[0][0](sl)
        for w, (ov, _) in zip(ws[1:], views[1:]):
            acc = acc + (w * inv) * ov(sl)
        out_ref[:, sl] = acc.astype(out_ref.dtype)


def combine_groups(outs, lses, dilations, *, tm=256):
    width = outs[0].shape[2]
    S = outs[0].shape[0] * outs[0].shape[1]
    tm = min(tm, S)
    assert S % tm == 0 and all(tm % (8 * d) == 0 for d in dilations)
    specs = [pl.BlockSpec((d, tm // d, width), lambda i: (0, i, 0)) for d in dilations]
    scratch = []
    for d in dilations:
        if d > 1:
            scratch += [pltpu.VMEM((width // LANES, tm, LANES), F32)] * 2
    return pl.pallas_call(
        functools.partial(_combine_kernel, dilations=tuple(dilations)),
        grid=(S // tm,),
        in_specs=specs + specs,
        out_specs=pl.BlockSpec((tm, width), lambda i: (i, 0)),
        out_shape=jax.ShapeDtypeStruct((S, width), BF16),
        scratch_shapes=scratch,
        compiler_params=_params("parallel"),
        name="combine_groups",
    )(*outs, *lses)


def kernel(x, positions, norm_mix, norm_mlp, norm_final, mlp_w_up, mlp_w_down, moba_w_qkv, moba_w_o,
           conv_w_in, conv_w, conv_w_out, dil_w_qkv, dil_w_o):
    B, S, D = x.shape
    assert B == 1 and D % HEAD_DIM == 0
    n_heads = D // HEAD_DIM
    depth = norm_mix.shape[0]
    h = x[0]
    tables = rope_tables(positions[0])
    mixers = ((moba_w_qkv, moba_w_o), (conv_w_in, conv_w_out), (dil_w_qkv, dil_w_o))

    def f32_weights(i):
        kind, j = i % N_MIXERS, i // N_MIXERS
        return [(mixers[kind][0], j), (mixers[kind][1], j), (mlp_w_up, i), (mlp_w_down, i)]

    weights = [w[idx:idx + 1].astype(BF16) for w, idx in f32_weights(0)]
    tn = D
    for i in range(depth):
        kind = i % N_MIXERS
        w_in, w_out, w_up, w_down = weights
        if kind == 0:
            qkv = norm_matmul_rope(h, norm_mix[i], w_in, 0, tables, n_out=3 * D, tn=tn)
            h = matmul_residual(moba_attention(qkv[0], n_heads), w_out, 0, h)
        elif kind == 1:
            a = conv_mix(h, norm_mix[i], w_in, 0, conv_w[i // N_MIXERS])
            h = matmul_residual([a], w_out, 0, h)
        else:
            outs, lses = [], []
            for g, (window, dilation) in enumerate(DIL_GROUPS):
                qkv = norm_matmul_rope(h, norm_mix[i], w_in, 0, tables, n_out=3 * D,
                                       col0=g * 3 * D // tn, dilation=dilation, tn=tn)
                o_g, lse_g = dilated_group_attention(qkv, window, dilation, n_heads)
                outs.append(o_g)
                lses.append(lse_g)
            o = combine_groups(outs, lses, [d for _, d in DIL_GROUPS])
            h = matmul_residual([o], w_out, 0, h)
        last = i == depth - 1
        h, weights = mlp_residual(h, norm_mlp[i], w_up, w_down, 0, norm_final if last else None,
                                  () if last else f32_weights(i + 1))
    return h[None]
```

```python
import functools
import math

import jax
import jax.numpy as jnp
from jax import lax
from jax.experimental import pallas as pl
from jax.experimental.pallas import tpu as pltpu

HEAD_DIM = 128
MOBA_BLOCK = 256
MOBA_TOP_K = 3
MOBA_CHUNK = 2
CONV_WIDTH = 3
DIL_GROUPS = ((128, 1), (512, 4), (2048, 16))
ROT_DIM = HEAD_DIM // 4
ROPE_THETA = 500000.0
RMS_EPS = 1e-5
NEG_INF = -1e30
N_MIXERS = 3

LANES = 128
VMEM_LIMIT = 56 * 1024 * 1024
MOBA_VMEM_LIMIT = 60 * 1024 * 1024

F32 = jnp.float32
BF16 = jnp.bfloat16


def _params(*sem):
    return pltpu.CompilerParams(dimension_semantics=sem, vmem_limit_bytes=VMEM_LIMIT)


def _rmsnorm_rows(x, g):
    y = x * lax.rsqrt(jnp.mean(x * x, axis=-1, keepdims=True) + RMS_EPS)
    return y * g


def _dot(a, b):
    return jnp.dot(a, b, preferred_element_type=F32)


def _dot_nt(a, b):
    return lax.dot_general(a, b, (((1,), (1,)), ((), ())), preferred_element_type=F32)


def _dot_tn(a, b):
    return lax.dot_general(a, b, (((0,), (0,)), ((), ())), preferred_element_type=F32)


def _lane_groups(x):
    return [x[:, g * LANES:(g + 1) * LANES] for g in range(x.shape[1] // LANES)]


def _rope_table_kernel(pos_ref, invf_ref, cos_ref, sa_ref, sb_ref):
    ang = pos_ref[...] * invf_ref[...]
    lane = lax.broadcasted_iota(jnp.int32, ang.shape, 1)
    c, s = jnp.cos(ang), jnp.sin(ang)
    half = ROT_DIM // 2
    cos_ref[...] = jnp.where(lane < ROT_DIM, c, 1.0)
    sa_ref[...] = jnp.where(lane < half, -s, 0.0)
    sb_ref[...] = jnp.where((lane >= half) & (lane < ROT_DIM), s, 0.0)


def rope_tables(positions):
    S = positions.shape[0]
    tm = min(S, 1024)
    half = ROT_DIM // 2
    inv_freq = ROPE_THETA ** (-jnp.arange(0, ROT_DIM, 2, dtype=F32) / ROT_DIM)
    invf = jnp.concatenate([inv_freq, inv_freq, jnp.zeros((LANES - 2 * half,), F32)])[None, :]
    pos = positions.astype(F32)[:, None]
    tab = jax.ShapeDtypeStruct((S, LANES), F32)
    return pl.pallas_call(
        _rope_table_kernel,
        grid=(S // tm,),
        in_specs=[pl.BlockSpec((tm, 1), lambda i: (i, 0)),
                  pl.BlockSpec((1, LANES), lambda i: (0, 0))],
        out_specs=[pl.BlockSpec((tm, LANES), lambda i: (i, 0))] * 3,
        out_shape=[tab, tab, tab],
        compiler_params=_params("arbitrary"),
        name="rope_tables",
    )(pos, invf)


def _apply_rope(t, cos, sa, sb):
    half = ROT_DIM // 2
    return (t * cos + pltpu.roll(t, LANES - half, 1) * sa + pltpu.roll(t, half, 1) * sb)


def _norm_mm_rope_kernel(x_ref, g_ref, w_ref, cos_ref, sa_ref, sb_ref, o_ref, xn_ref, *scratch,
                         tn, d_model, dilation):
    j = pl.program_id(1)

    @pl.when(j == 0)
    def _():
        xn_ref[...] = _rmsnorm_rows(x_ref[...], g_ref[...]).astype(BF16)

    acc = _dot(xn_ref[...], w_ref[...])
    roped = ((j * tn) // d_model) % 3 != 2
    heads = tn // HEAD_DIM
    if dilation == 1:
        cos = jnp.where(roped, cos_ref[...], 1.0)
        sa = jnp.where(roped, sa_ref[...], 0.0)
        sb = jnp.where(roped, sb_ref[...], 0.0)
        for c in range(heads):
            sl = slice(c * HEAD_DIM, (c + 1) * HEAD_DIM)
            o_ref[0, :, sl] = _apply_rope(acc[:, sl], cos, sa, sb).astype(o_ref.dtype)
    else:
        acc_ref, = scratch
        for c in range(heads):
            acc_ref[c] = acc[:, c * HEAD_DIM:(c + 1) * HEAD_DIM]
        rows = acc.shape[0] // dilation
        for r in range(dilation):
            pick = pl.ds(r, rows, stride=dilation)
            cos = jnp.where(roped, cos_ref[pick, :], 1.0)
            sa = jnp.where(roped, sa_ref[pick, :], 0.0)
            sb = jnp.where(roped, sb_ref[pick, :], 0.0)
            for c in range(heads):
                sl = slice(c * HEAD_DIM, (c + 1) * HEAD_DIM)
                o_ref[r, :, sl] = _apply_rope(acc_ref[c, pick, :], cos, sa, sb).astype(o_ref.dtype)


def norm_matmul_rope(x, g, w_all, layer, tables, *, n_out, col0=0, dilation=1, tm=512, tn=2048):
    S, D = x.shape
    tm = min(tm, S)
    assert S % tm == 0 and n_out % tn == 0 and D % tn == 0 and tm % (16 * dilation) == 0
    rows = tm // dilation
    scratch = [pltpu.VMEM((tm, D), BF16)]
    if dilation > 1:
        scratch.append(pltpu.VMEM((tn // HEAD_DIM, tm, HEAD_DIM), F32))
    return pl.pallas_call(
        functools.partial(_norm_mm_rope_kernel, tn=tn, d_model=D, dilation=dilation),
        grid=(S // tm, n_out // tn),
        in_specs=[pl.BlockSpec((tm, D), lambda i, j: (i, 0)),
                  pl.BlockSpec((1, D), lambda i, j: (0, 0)),
                  pl.BlockSpec((None, D, tn), lambda i, j: (layer, 0, col0 + j))]
                 + [pl.BlockSpec((tm, LANES), lambda i, j: (i, 0))] * 3,
        out_specs=pl.BlockSpec((dilation, rows, tn), lambda i, j: (0, i, j)),
        out_shape=jax.ShapeDtypeStruct((dilation, S // dilation, n_out), BF16),
        scratch_shapes=scratch,
        compiler_params=_params("parallel", "arbitrary"),
        name="norm_matmul_rope",
    )(x, g.reshape(1, D), w_all, *tables)


def _mm_res_kernel(*refs, n_parts, tiles_per_part):
    a_refs, (w_ref, h_ref, o_ref) = refs[:n_parts], refs[n_parts:]
    part = pl.program_id(0) // tiles_per_part
    for p, a_ref in enumerate(a_refs):
        @pl.when(part == p)
        def _(a_ref=a_ref):
            o_ref[...] = h_ref[...] + _dot(a_ref[...], w_ref[...])


def matmul_residual(a_parts, w_all, layer, h, *, tm=1024, tn=1024):
    n_parts = len(a_parts)
    Sp, K = a_parts[0].shape
    S, N = h.shape
    tm, tn = min(tm, Sp), min(tn, N)
    assert Sp * n_parts == S and Sp % tm == 0 and N % tn == 0
    tiles = Sp // tm

    def a_spec(p):
        return pl.BlockSpec((tm, K), lambda i, j: (jnp.clip(i - p * tiles, 0, tiles - 1), 0))

    return pl.pallas_call(
        functools.partial(_mm_res_kernel, n_parts=n_parts, tiles_per_part=tiles),
        grid=(S // tm, N // tn),
        in_specs=[a_spec(p) for p in range(n_parts)]
                 + [pl.BlockSpec((None, K, tn), lambda i, j: (layer, 0, j)),
                    pl.BlockSpec((tm, tn), lambda i, j: (i, j))],
        out_specs=pl.BlockSpec((tm, tn), lambda i, j: (i, j)),
        out_shape=jax.ShapeDtypeStruct((S, N), F32),
        compiler_params=_params("parallel", "arbitrary"),
        name="matmul_residual",
    )(*a_parts, w_all, h)


def _mlp_kernel(*refs, final_norm, n_cast):
    h_ref, g_ref, wu_ref, wd_ref, gf_ref = refs[:5]
    src_refs = refs[5:5 + n_cast]
    o_ref = refs[5 + n_cast]
    dst_refs = refs[6 + n_cast:6 + 2 * n_cast]
    xn_ref = refs[6 + 2 * n_cast]
    f = pl.program_id(1)

    @pl.when(f == 0)
    def _():
        h = h_ref[...]
        xn_ref[...] = _rmsnorm_rows(h, g_ref[...]).astype(BF16)
        o_ref[...] = h

    hid = jnp.maximum(_dot(xn_ref[...], wu_ref[...]), 0.0)
    o_ref[...] += _dot((hid * hid).astype(BF16), wd_ref[...])

    if final_norm:
        @pl.when(f == pl.num_programs(1) - 1)
        def _():
            o_ref[...] = _rmsnorm_rows(o_ref[...], gf_ref[...])

    for src_ref, dst_ref in zip(src_refs, dst_refs):
        dst_ref[...] = src_ref[...].astype(dst_ref.dtype)


def mlp_residual(h, g, w_up_all, w_down_all, layer, g_final=None, cast_next=(), *, tm=512, tf=1024):
    S, D = h.shape
    Fd = w_up_all.shape[2]
    tm, tf = min(tm, S), min(tf, Fd)
    assert S % tm == 0 and Fd % tf == 0
    nf = Fd // tf
    n_steps = (S // tm) * nf
    final_norm = g_final is not None
    gf = (g_final if final_norm else g).reshape(1, D)
    cast_in, cast_out, cast_shapes = [], [], []
    for w_all, idx in cast_next:
        _, K, N = w_all.shape
        rows = K // n_steps
        assert rows * n_steps == K and rows % 16 == 0
        cast_in.append(pl.BlockSpec((None, rows, N), lambda i, f, idx=idx: (idx, i * nf + f, 0)))
        cast_out.append(pl.BlockSpec((None, rows, N), lambda i, f: (0, i * nf + f, 0)))
        cast_shapes.append(jax.ShapeDtypeStruct((1, K, N), BF16))
    outs = pl.pallas_call(
        functools.partial(_mlp_kernel, final_norm=final_norm, n_cast=len(cast_next)),
        grid=(S // tm, nf),
        in_specs=[pl.BlockSpec((tm, D), lambda i, f: (i, 0)),
                  pl.BlockSpec((1, D), lambda i, f: (0, 0)),
                  pl.BlockSpec((None, D, tf), lambda i, f: (layer, 0, f)),
                  pl.BlockSpec((None, tf, D), lambda i, f: (layer, f, 0)),
                  pl.BlockSpec((1, D), lambda i, f: (0, 0))] + cast_in,
        out_specs=[pl.BlockSpec((tm, D), lambda i, f: (i, 0))] + cast_out,
        out_shape=[jax.ShapeDtypeStruct((S, D), F32)] + cast_shapes,
        scratch_shapes=[pltpu.VMEM((tm, D), BF16)],
        compiler_params=_params("arbitrary", "arbitrary"),
        name="mlp_residual",
    )(h, g.reshape(1, D), w_up_all, w_down_all, gf, *[w for w, _ in cast_next])
    return outs[0], list(outs[1:])


def _moba_gate_kernel(q_ref, k_ref, qaug_ref, kmean_ref, *, nb, tq):
    blk = MOBA_BLOCK
    i = pl.program_id(1)

    @pl.when(i == 0)
    def _():
        for n in range(nb):
            kn = k_ref[n * blk:(n + 1) * blk, :].astype(F32)
            kmean_ref[n:n + 1, :] = jnp.mean(kn, axis=0, keepdims=True)

    q = q_ref[...]
    km = kmean_ref[...]
    km_hi = km.astype(BF16)
    r1 = km - km_hi.astype(F32)
    km_mid = r1.astype(BF16)
    km_lo = (r1 - km_mid.astype(F32)).astype(BF16)
    gate = _dot_nt(km_hi, q) + _dot_nt(km_mid, q) + _dot_nt(km_lo, q)

    blk_id = lax.broadcasted_iota(jnp.int32, (nb, tq), 0)
    tok = lax.broadcasted_iota(jnp.int32, (nb, tq), 1)
    qb = i * (tq // blk) + lax.shift_right_logical(tok, jnp.int32(blk.bit_length() - 1))
    gate = jnp.where(blk_id < qb, gate, NEG_INF)
    chosen = jnp.zeros((nb, tq), jnp.bool_)
    for n in range(min(MOBA_TOP_K, nb)):
        best = jnp.max(gate, axis=0, keepdims=True)
        idx = jnp.min(jnp.where(gate == best, blk_id, nb), axis=0, keepdims=True)
        hit = blk_id == idx
        chosen = chosen | (hit & (n < qb))
        gate = jnp.where(hit, -jnp.inf, gate)

    eye = (lax.broadcasted_iota(jnp.int32, (nb, LANES), 0)
           == lax.broadcasted_iota(jnp.int32, (nb, LANES), 1))
    picked = _dot_tn(jnp.where(chosen, 1.0, 0.0).astype(BF16), jnp.where(eye, 1.0, 0.0).astype(BF16))
    lane = lax.broadcasted_iota(jnp.int32, (tq, LANES), 1)
    qaug_ref[:, 0:HEAD_DIM] = q
    qaug_ref[:, HEAD_DIM:2 * HEAD_DIM] = jnp.where((picked > 0.5) | (lane >= nb), 0.0, NEG_INF).astype(BF16)


def _moba_slots(nb, chunk):
    return max(-(-p // chunk) + -(-(nb - 1 - p) // chunk) for p in range(nb // 2))


def _moba_attn_kernel(qlo_ref, qhi_ref, k_ref, v_ref, olo_ref, ohi_ref, ka_ref, va_ref, q_ref, s_ref, own_ref,
                      m_ref, acc_ref, *, chunk, nb, heads):
    blk = MOBA_BLOCK
    span = chunk * blk
    last_chunk = nb // chunk - 1
    n_slots = _moba_slots(nb, chunk)
    p = pl.program_id(1)
    hd2 = 2 * HEAD_DIM

    @pl.when(jnp.logical_and(pl.program_id(0) == 0, p == 0))
    def _():
        lane = lax.broadcasted_iota(jnp.int32, (blk, LANES), 1)
        ones_col = jnp.where(lane == 0, 1.0, 0.0).astype(BF16)
        for n in range(nb):
            rows = slice(n * blk, (n + 1) * blk)
            for g in range(heads):
                ka_ref[g, rows, HEAD_DIM:hd2] = jnp.where(lane == n, 1.0, 0.0).astype(BF16)
                va_ref[g, rows, HEAD_DIM:hd2] = ones_col

    @pl.when(p == 0)
    def _():
        for g in range(heads):
            ka_ref[g, :, 0:HEAD_DIM] = k_ref[:, g * HEAD_DIM:(g + 1) * HEAD_DIM]
            va_ref[g, :, 0:HEAD_DIM] = v_ref[:, g * HEAD_DIM:(g + 1) * HEAD_DIM]

    q_ref[0] = qlo_ref[...]
    q_ref[1] = qhi_ref[...]

    qbs = (p, nb - 1 - p)
    q_refs, o_refs = (qlo_ref, qhi_ref), (olo_ref, ohi_ref)
    log2_scale = HEAD_DIM ** -0.5 * math.log2(math.e)
    n_lo = (qbs[0] + chunk - 1) // chunk
    n_hi = (qbs[1] + chunk - 1) // chunk

    def slot(c):
        is_hi = c < n_hi
        is_lo = jnp.logical_and(c >= n_hi, c < n_hi + n_lo)
        ck = jnp.where(is_hi, c, jnp.where(is_lo, c - n_hi, last_chunk))
        return is_hi, jnp.where(is_hi, 1, 0), pl.multiple_of(ck * span, span)

    row = lax.broadcasted_iota(jnp.int32, (blk, blk), 0)
    col = lax.broadcasted_iota(jnp.int32, (blk, blk), 1)
    starts = [pl.multiple_of(qb * blk, blk) for qb in qbs]
    for t in range(2):
        for g in range(heads):
            q = q_refs[t][:, g * hd2:g * hd2 + HEAD_DIM]
            s = _dot_nt(q, ka_ref[g, pl.ds(starts[t], blk), 0:HEAD_DIM]) * log2_scale
            s = jnp.where(col <= row, s, NEG_INF)
            own_ref[g, t] = s
            m_ref[g, t] = functools.reduce(jnp.maximum, _lane_groups(s))
    for c in range(n_slots):
        is_hi, which, st = slot(c)
        for g in range(heads):
            cols = slice(g * hd2, (g + 1) * hd2)
            s = _dot_nt(q_ref[which, :, cols], ka_ref[g, pl.ds(st, span), :]) * log2_scale
            s_ref[g, c] = s
            m_ref[g, which] = functools.reduce(jnp.maximum, _lane_groups(s), m_ref[g, which])
    for t in range(2):
        for g in range(heads):
            m_ref[g, t] = jnp.broadcast_to(jnp.max(m_ref[g, t], axis=1, keepdims=True), (blk, LANES))

    for t in range(2):
        for g in range(heads):
            m = m_ref[g, t]
            pr = [jnp.exp2(s - m) for s in _lane_groups(own_ref[g, t])]
            acc_ref[g, t] = _dot(jnp.concatenate(pr, axis=1).astype(BF16), va_ref[g, pl.ds(starts[t], blk), :])
    for c in range(n_slots):
        _, which, st = slot(c)
        for g in range(heads):
            m = m_ref[g, which]
            pr = [jnp.exp2(s - m) for s in _lane_groups(s_ref[g, c])]
            acc_ref[g, which] += _dot(jnp.concatenate(pr, axis=1).astype(BF16), va_ref[g, pl.ds(st, span), :])
    for t in range(2):
        for g in range(heads):
            acc = acc_ref[g, t]
            out = acc[:, 0:HEAD_DIM] * (1.0 / acc[:, HEAD_DIM:HEAD_DIM + 1])
            o_refs[t][:, g * HEAD_DIM:(g + 1) * HEAD_DIM] = out.astype(o_refs[t].dtype)


def moba_attention(qkv, n_heads, *, tq=4096, heads=2):
    S = qkv.shape[0]
    blk, hd, H, chunk = MOBA_BLOCK, HEAD_DIM, n_heads, MOBA_CHUNK
    tq = min(tq, S)
    assert S % tq == 0 and tq % blk == 0 and hd == LANES and blk & (blk - 1) == 0
    nb = S // blk
    assert nb <= LANES and nb % 8 == 0 and nb % (2 * chunk) == 0
    half = nb // 2
    qaug = pl.pallas_call(
        functools.partial(_moba_gate_kernel, nb=nb, tq=tq),
        grid=(H, S // tq),
        in_specs=[pl.BlockSpec((tq, hd), lambda h, i: (i, h)),
                  pl.BlockSpec((S, hd), lambda h, i: (0, H + h))],
        out_specs=pl.BlockSpec((tq, 2 * hd), lambda h, i: (i, h)),
        out_shape=jax.ShapeDtypeStruct((S, H * 2 * hd), BF16),
        scratch_shapes=[pltpu.VMEM((nb, hd), F32)],
        compiler_params=_params("parallel", "arbitrary"),
        name="moba_gate",
    )(qkv, qkv)
    out = jax.ShapeDtypeStruct((S // 2, H * hd), BF16)
    G = heads
    assert H % G == 0
    return pl.pallas_call(
        functools.partial(_moba_attn_kernel, chunk=chunk, nb=nb, heads=G),
        grid=(H // G, half),
        in_specs=[pl.BlockSpec((blk, G * 2 * hd), lambda h, p: (p, h)),
                  pl.BlockSpec((blk, G * 2 * hd), lambda h, p: (nb - 1 - p, h)),
                  pl.BlockSpec((S, G * hd), lambda h, p: (0, H // G + h)),
                  pl.BlockSpec((S, G * hd), lambda h, p: (0, 2 * (H // G) + h))],
        out_specs=[pl.BlockSpec((blk, G * hd), lambda h, p: (p, h)),
                   pl.BlockSpec((blk, G * hd), lambda h, p: (half - 1 - p, h))],
        out_shape=[out, out],
        scratch_shapes=[pltpu.VMEM((G, S, 2 * hd), BF16),
                        pltpu.VMEM((G, S, 2 * hd), BF16),
                        pltpu.VMEM((2, blk, G * 2 * hd), BF16),
                        pltpu.VMEM((G, _moba_slots(nb, chunk), blk, chunk * blk), F32),
                        pltpu.VMEM((G, 2, blk, blk), F32),
                        pltpu.VMEM((G, 2, blk, LANES), F32),
                        pltpu.VMEM((G, 2, blk, 2 * hd), F32)],
        compiler_params=pltpu.CompilerParams(dimension_semantics=("arbitrary", "arbitrary"),
                                             vmem_limit_bytes=MOBA_VMEM_LIMIT),
        name="moba_attention",
    )(qaug, qaug, qkv, qkv)


TAIL = 8


def _conv_mix_kernel(x_ref, g_ref, wb_ref, wc_ref, wu_ref, cw_ref, a_ref, xn_ref, tail_ref):
    i, j = pl.program_id(0), pl.program_id(1)

    @pl.when(j == 0)
    def _():
        xn_ref[...] = _rmsnorm_rows(x_ref[...], g_ref[...]).astype(BF16)

    xn = xn_ref[...]
    z = _dot(xn, wc_ref[...]) * _dot(xn, wu_ref[...])
    tm = z.shape[0]
    prev = jnp.where(i == 0, 0.0, tail_ref[j])
    row = lax.broadcasted_iota(jnp.int32, z.shape, 0)
    zm1 = jnp.where(row == 0, prev[TAIL - 1:TAIL, :], pltpu.roll(z, 1, 0))
    zm2 = jnp.where(row == 0, prev[TAIL - 2:TAIL - 1, :],
                    jnp.where(row == 1, prev[TAIL - 1:TAIL, :], pltpu.roll(z, 2, 0)))
    cw = cw_ref[...]
    conv = cw[0:1, :] * zm2 + cw[1:2, :] * zm1 + cw[2:3, :] * z
    a_ref[...] = (_dot(xn, wb_ref[...]) * conv).astype(a_ref.dtype)
    tail_ref[j] = z[tm - TAIL:tm, :]


def conv_mix(x, g, w_in_all, layer, conv_w, *, tm=512, tn=512):
    S, D = x.shape
    tm, tn = min(tm, S), min(tn, D)
    assert S % tm == 0 and D % tn == 0 and w_in_all.shape[2] == 3 * D and tm % TAIL == 0
    assert conv_w.shape == (CONV_WIDTH, D)
    nj = D // tn
    return pl.pallas_call(
        _conv_mix_kernel,
        grid=(S // tm, nj),
        in_specs=[pl.BlockSpec((tm, D), lambda i, j: (i, 0)),
                  pl.BlockSpec((1, D), lambda i, j: (0, 0)),
                  pl.BlockSpec((None, D, tn), lambda i, j: (layer, 0, j)),
                  pl.BlockSpec((None, D, tn), lambda i, j: (layer, 0, nj + j)),
                  pl.BlockSpec((None, D, tn), lambda i, j: (layer, 0, 2 * nj + j)),
                  pl.BlockSpec((CONV_WIDTH, tn), lambda i, j: (0, j))],
        out_specs=pl.BlockSpec((tm, tn), lambda i, j: (i, j)),
        out_shape=jax.ShapeDtypeStruct((S, D), BF16),
        scratch_shapes=[pltpu.VMEM((tm, D), BF16), pltpu.VMEM((nj, TAIL, tn), F32)],
        compiler_params=_params("arbitrary", "arbitrary"),
        name="conv_mix",
    )(x, g.reshape(1, D), w_in_all, w_in_all, w_in_all, conv_w)


def _dilated_kernel(q_ref, k_ref, v_ref, o_ref, lse_ref, kband_ref, vband_ref, *, band, n_heads):
    j = pl.program_id(1)
    scale = HEAD_DIM ** -0.5

    @pl.when(j == 0)
    def _():
        kband_ref[0:band, :] = jnp.zeros((band, kband_ref.shape[1]), BF16)
        vband_ref[0:band, :] = jnp.zeros((band, vband_ref.shape[1]), BF16)

    kband_ref[band:2 * band, :] = k_ref[...]
    vband_ref[band:2 * band, :] = v_ref[...]
    a = lax.broadcasted_iota(jnp.int32, (band, 2 * band), 0)
    b = lax.broadcasted_iota(jnp.int32, (band, 2 * band), 1)
    valid = (b >= a) & (b <= a + band) & ((j > 0) | (b >= band))
    for hh in range(n_heads):
        sl = slice(hh * HEAD_DIM, (hh + 1) * HEAD_DIM)
        s = _dot_nt(q_ref[:, sl], kband_ref[:, sl]) * scale
        s = jnp.where(valid, s, NEG_INF)
        m = jnp.max(s, axis=1, keepdims=True)
        e = jnp.exp(s - m)
        l = jnp.sum(e, axis=1, keepdims=True)
        o_ref[:, sl] = (_dot(e.astype(BF16), vband_ref[:, sl]) * (1.0 / l)).astype(o_ref.dtype)
        lse_ref[:, sl] = jnp.broadcast_to(m + jnp.log(l), (band, HEAD_DIM))
    kband_ref[0:band, :] = k_ref[...]
    vband_ref[0:band, :] = v_ref[...]


def dilated_group_attention(qkv, window, dilation, n_heads):
    dil, L, W = qkv.shape
    band = window // dilation
    width = n_heads * HEAD_DIM
    assert dil == dilation and L % band == 0 and W == 3 * width
    nbl = L // band

    def spec(t):
        return pl.BlockSpec((None, band, width), lambda r, j: (r, j, t))

    return pl.pallas_call(
        functools.partial(_dilated_kernel, band=band, n_heads=n_heads),
        grid=(dil, nbl),
        in_specs=[spec(0), spec(1), spec(2)],
        out_specs=[pl.BlockSpec((None, band, width), lambda r, j: (r, j, 0))] * 2,
        out_shape=[jax.ShapeDtypeStruct((dil, L, width), BF16),
                   jax.ShapeDtypeStruct((dil, L, width), F32)],
        scratch_shapes=[pltpu.VMEM((2 * band, width), BF16)] * 2,
        compiler_params=_params("parallel", "arbitrary"),
        name="dilated_attention",
    )(qkv, qkv, qkv)


def _combine_kernel(*refs, dilations):
    n = len(dilations)
    o_refs, l_refs, out_ref = refs[:n], refs[n:2 * n], refs[2 * n]
    scratch = list(refs[2 * n + 1:])
    tm, width = out_ref.shape

    views = []
    for g, dil in enumerate(dilations):
        if dil == 1:
            views.append((lambda sl, o=o_refs[g]: o[0, :, sl].astype(F32), lambda sl, l=l_refs[g]: l[0, :, sl]))
            continue
        o_scr, l_scr = scratch.pop(0), scratch.pop(0)
        rows = tm // dil
        for c in range(width // LANES):
            sl = slice(c * LANES, (c + 1) * LANES)
            for r in range(dil):
                o_scr[c, pl.ds(r, rows, stride=dil), :] = o_refs[g][r, :, sl].astype(F32)
                l_scr[c, pl.ds(r, rows, stride=dil), :] = l_refs[g][r, :, sl]
        views.append((lambda sl, s=o_scr: s[sl.start // LANES], lambda sl, s=l_scr: s[sl.start // LANES]))

    for c in range(width // LANES):
        sl = slice(c * LANES, (c + 1) * LANES)
        lses = [lv(sl) for _, lv in views]
        top = functools.reduce(jnp.maximum, lses)
        ws = [jnp.exp(l - top) for l in lses]
        inv = 1.0 / functools.reduce(jnp.add, ws)
        acc = (ws[0] * inv) * views[0][0](sl)
        for w, (ov, _) in zip(ws[1:], views[1:]):
            acc = acc + (w * inv) * ov(sl)
        out_ref[:, sl] = acc.astype(out_ref.dtype)


def combine_groups(outs, lses, dilations, *, tm=256):
    width = outs[0].shape[2]
    S = outs[0].shape[0] * outs[0].shape[1]
    tm = min(tm, S)
    assert S % tm == 0 and all(tm % (8 * d) == 0 for d in dilations)
    specs = [pl.BlockSpec((d, tm // d, width), lambda i: (0, i, 0)) for d in dilations]
    scratch = []
    for d in dilations:
        if d > 1:
            scratch += [pltpu.VMEM((width // LANES, tm, LANES), F32)] * 2
    return pl.pallas_call(
        functools.partial(_combine_kernel, dilations=tuple(dilations)),
        grid=(S // tm,),
        in_specs=specs + specs,
        out_specs=pl.BlockSpec((tm, width), lambda i: (i, 0)),
        out_shape=jax.ShapeDtypeStruct((S, width), BF16),
        scratch_shapes=scratch,
        compiler_params=_params("parallel"),
        name="combine_groups",
    )(*outs, *lses)


def kernel(x, positions, norm_mix, norm_mlp, norm_final, mlp_w_up, mlp_w_down, moba_w_qkv, moba_w_o,
           conv_w_in, conv_w, conv_w_out, dil_w_qkv, dil_w_o):
    B, S, D = x.shape
    assert B == 1 and D % HEAD_DIM == 0
    n_heads = D // HEAD_DIM
    depth = norm_mix.shape[0]
    h = x[0]
    tables = rope_tables(positions[0])
    mixers = ((moba_w_qkv, moba_w_o), (conv_w_in, conv_w_out), (dil_w_qkv, dil_w_o))

    def f32_weights(i):
        kind, j = i % N_MIXERS, i // N_MIXERS
        return [(mixers[kind][0], j), (mixers[kind][1], j), (mlp_w_up, i), (mlp_w_down, i)]

    weights = [w[idx:idx + 1].astype(BF16) for w, idx in f32_weights(0)]
    tn = D
    for i in range(depth):
        kind = i % N_MIXERS
        w_in, w_out, w_up, w_down = weights
        if kind == 0:
            qkv = norm_matmul_rope(h, norm_mix[i], w_in, 0, tables, n_out=3 * D, tn=tn)
            h = matmul_residual(moba_attention(qkv[0], n_heads), w_out, 0, h)
        elif kind == 1:
            a = conv_mix(h, norm_mix[i], w_in, 0, conv_w[i // N_MIXERS])
            h = matmul_residual([a], w_out, 0, h)
        else:
            outs, lses = [], []
            for g, (window, dilation) in enumerate(DIL_GROUPS):
                qkv = norm_matmul_rope(h, norm_mix[i], w_in, 0, tables, n_out=3 * D,
                                       col0=g * 3 * D // tn, dilation=dilation, tn=tn)
                o_g, lse_g = dilated_group_attention(qkv, window, dilation, n_heads)
                outs.append(o_g)
                lses.append(lse_g)
            o = combine_groups(outs, lses, [d for _, d in DIL_GROUPS])
            h = matmul_residual([o], w_out, 0, h)
        last = i == depth - 1
        h, weights = mlp_residual(h, norm_mlp[i], w_up, w_down, 0, norm_final if last else None,
                                  () if last else f32_weights(i + 1))
    return h[None]
```

```python
import functools
import math

import jax
import jax.numpy as jnp
from jax import lax
from jax.experimental import pallas as pl
from jax.experimental.pallas import tpu as pltpu

HEAD_DIM = 128
MOBA_BLOCK = 256
MOBA_TOP_K = 3
MOBA_CHUNK = 2
CONV_WIDTH = 3
DIL_GROUPS = ((128, 1), (512, 4), (2048, 16))
ROT_DIM = HEAD_DIM // 4
ROPE_THETA = 500000.0
RMS_EPS = 1e-5
NEG_INF = -1e30
N_MIXERS = 3

LANES = 128
VMEM_LIMIT = 56 * 1024 * 1024
MOBA_VMEM_LIMIT = 60 * 1024 * 1024

F32 = jnp.float32
BF16 = jnp.bfloat16


def _params(*sem):
    return pltpu.CompilerParams(dimension_semantics=sem, vmem_limit_bytes=VMEM_LIMIT)


def _rmsnorm_rows(x, g):
    y = x * lax.rsqrt(jnp.mean(x * x, axis=-1, keepdims=True) + RMS_EPS)
    return y * g


def _dot(a, b):
    return jnp.dot(a, b, preferred_element_type=F32)


def _dot_nt(a, b):
    return lax.dot_general(a, b, (((1,), (1,)), ((), ())), preferred_element_type=F32)


def _dot_tn(a, b):
    return lax.dot_general(a, b, (((0,), (0,)), ((), ())), preferred_element_type=F32)


def _lane_groups(x):
    return [x[:, g * LANES:(g + 1) * LANES] for g in range(x.shape[1] // LANES)]


def _rope_table_kernel(pos_ref, invf_ref, cos_ref, sa_ref, sb_ref):
    ang = pos_ref[...] * invf_ref[...]
    lane = lax.broadcasted_iota(jnp.int32, ang.shape, 1)
    c, s = jnp.cos(ang), jnp.sin(ang)
    half = ROT_DIM // 2
    cos_ref[...] = jnp.where(lane < ROT_DIM, c, 1.0)
    sa_ref[...] = jnp.where(lane < half, -s, 0.0)
    sb_ref[...] = jnp.where((lane >= half) & (lane < ROT_DIM), s, 0.0)


def rope_tables(positions):
    S = positions.shape[0]
    tm = min(S, 1024)
    half = ROT_DIM // 2
    inv_freq = ROPE_THETA ** (-jnp.arange(0, ROT_DIM, 2, dtype=F32) / ROT_DIM)
    invf = jnp.concatenate([inv_freq, inv_freq, jnp.zeros((LANES - 2 * half,), F32)])[None, :]
    pos = positions.astype(F32)[:, None]
    tab = jax.ShapeDtypeStruct((S, LANES), F32)
    return pl.pallas_call(
        _rope_table_kernel,
        grid=(S // tm,),
        in_specs=[pl.BlockSpec((tm, 1), lambda i: (i, 0)),
                  pl.BlockSpec((1, LANES), lambda i: (0, 0))],
        out_specs=[pl.BlockSpec((tm, LANES), lambda i: (i, 0))] * 3,
        out_shape=[tab, tab, tab],
        compiler_params=_params("arbitrary"),
        name="rope_tables",
    )(pos, invf)


def _apply_rope(t, cos, sa, sb):
    half = ROT_DIM // 2
    return (t * cos + pltpu.roll(t, LANES - half, 1) * sa + pltpu.roll(t, half, 1) * sb)


def _norm_mm_rope_kernel(x_ref, g_ref, wr_ref, wv_ref, cos_ref, sa_ref, sb_ref, oqk_ref, ov_ref, xn_ref,
                         *scratch, dilation):
    @pl.when(pl.program_id(1) == 0)
    def _():
        xn_ref[...] = _rmsnorm_rows(x_ref[...], g_ref[...]).astype(BF16)

    xn = xn_ref[...]
    acc_r = _dot(xn, wr_ref[...])
    acc_v = _dot(xn, wv_ref[...])
    heads_r, heads_v = acc_r.shape[1] // HEAD_DIM, acc_v.shape[1] // HEAD_DIM
    if dilation == 1:
        cos, sa, sb = cos_ref[...], sa_ref[...], sb_ref[...]
        for c in range(heads_r):
            sl = slice(c * HEAD_DIM, (c + 1) * HEAD_DIM)
            oqk_ref[0, :, sl] = _apply_rope(acc_r[:, sl], cos, sa, sb).astype(oqk_ref.dtype)
        ov_ref[0] = acc_v.astype(ov_ref.dtype)
    else:
        acc_ref, = scratch
        for c in range(heads_r):
            acc_ref[c] = acc_r[:, c * HEAD_DIM:(c + 1) * HEAD_DIM]
        for c in range(heads_v):
            acc_ref[heads_r + c] = acc_v[:, c * HEAD_DIM:(c + 1) * HEAD_DIM]
        rows = acc_r.shape[0] // dilation
        for r in range(dilation):
            pick = pl.ds(r, rows, stride=dilation)
            cos, sa, sb = cos_ref[pick, :], sa_ref[pick, :], sb_ref[pick, :]
            for c in range(heads_r):
                sl = slice(c * HEAD_DIM, (c + 1) * HEAD_DIM)
                oqk_ref[r, :, sl] = _apply_rope(acc_ref[c, pick, :], cos, sa, sb).astype(oqk_ref.dtype)
        for r in range(dilation):
            pick = pl.ds(r, rows, stride=dilation)
            for c in range(heads_v):
                sl = slice(c * HEAD_DIM, (c + 1) * HEAD_DIM)
                ov_ref[r, :, sl] = acc_ref[heads_r + c, pick, :].astype(ov_ref.dtype)


def norm_matmul_rope(x, g, w_all, layer, tables, *, sec0=0, dilation=1, tm=512):
    S, D = x.shape
    tm = min(tm, S)
    assert S % tm == 0 and tm % (16 * dilation) == 0 and D % (2 * HEAD_DIM) == 0
    rows = tm // dilation
    half = D // 2
    scratch = [pltpu.VMEM((tm, D), BF16)]
    if dilation > 1:
        scratch.append(pltpu.VMEM(((D + half) // HEAD_DIM, tm, HEAD_DIM), F32))
    L = S // dilation
    return pl.pallas_call(
        functools.partial(_norm_mm_rope_kernel, dilation=dilation),
        grid=(S // tm, 2),
        in_specs=[pl.BlockSpec((tm, D), lambda i, j: (i, 0)),
                  pl.BlockSpec((1, D), lambda i, j: (0, 0)),
                  pl.BlockSpec((None, D, D), lambda i, j: (layer, 0, sec0 + j)),
                  pl.BlockSpec((None, D, half), lambda i, j: (layer, 0, 2 * (sec0 + 2) + j))]
                 + [pl.BlockSpec((tm, LANES), lambda i, j: (i, 0))] * 3,
        out_specs=[pl.BlockSpec((dilation, rows, D), lambda i, j: (0, i, j)),
                   pl.BlockSpec((dilation, rows, half), lambda i, j: (0, i, j))],
        out_shape=[jax.ShapeDtypeStruct((dilation, L, 2 * D), BF16),
                   jax.ShapeDtypeStruct((dilation, L, D), BF16)],
        scratch_shapes=scratch,
        compiler_params=_params("parallel", "arbitrary"),
        name="norm_matmul_rope",
    )(x, g.reshape(1, D), w_all, w_all, *tables)


def _mm_res_kernel(*refs, n_parts, tiles_per_part):
    a_refs, (w_ref, h_ref, o_ref) = refs[:n_parts], refs[n_parts:]
    part = pl.program_id(0) // tiles_per_part
    for p, a_ref in enumerate(a_refs):
        @pl.when(part == p)
        def _(a_ref=a_ref):
            o_ref[...] = h_ref[...] + _dot(a_ref[...], w_ref[...])


def matmul_residual(a_parts, w_all, layer, h, *, tm=1024, tn=1024):
    n_parts = len(a_parts)
    Sp, K = a_parts[0].shape
    S, N = h.shape
    tm, tn = min(tm, Sp), min(tn, N)
    assert Sp * n_parts == S and Sp % tm == 0 and N % tn == 0
    tiles = Sp // tm

    def a_spec(p):
        return pl.BlockSpec((tm, K), lambda i, j: (jnp.clip(i - p * tiles, 0, tiles - 1), 0))

    return pl.pallas_call(
        functools.partial(_mm_res_kernel, n_parts=n_parts, tiles_per_part=tiles),
        grid=(S // tm, N // tn),
        in_specs=[a_spec(p) for p in range(n_parts)]
                 + [pl.BlockSpec((None, K, tn), lambda i, j: (layer, 0, j)),
                    pl.BlockSpec((tm, tn), lambda i, j: (i, j))],
        out_specs=pl.BlockSpec((tm, tn), lambda i, j: (i, j)),
        out_shape=jax.ShapeDtypeStruct((S, N), F32),
        compiler_params=_params("parallel", "arbitrary"),
        name="matmul_residual",
    )(*a_parts, w_all, h)


def _mlp_kernel(*refs, final_norm, n_cast):
    h_ref, g_ref, wu_ref, wd_ref, gf_ref = refs[:5]
    src_refs = refs[5:5 + n_cast]
    o_ref = refs[5 + n_cast]
    dst_refs = refs[6 + n_cast:6 + 2 * n_cast]
    xn_ref = refs[6 + 2 * n_cast]
    f = pl.program_id(1)

    @pl.when(f == 0)
    def _():
        h = h_ref[...]
        xn_ref[...] = _rmsnorm_rows(h, g_ref[...]).astype(BF16)
        o_ref[...] = h

    hid = jnp.maximum(_dot(xn_ref[...], wu_ref[...]), 0.0)
    o_ref[...] += _dot((hid * hid).astype(BF16), wd_ref[...])

    if final_norm:
        @pl.when(f == pl.num_programs(1) - 1)
        def _():
            o_ref[...] = _rmsnorm_rows(o_ref[...], gf_ref[...])

    for src_ref, dst_ref in zip(src_refs, dst_refs):
        dst_ref[...] = src_ref[...].astype(dst_ref.dtype)


def mlp_residual(h, g, w_up_all, w_down_all, layer, g_final=None, cast_next=(), *, tm=512, tf=1024):
    S, D = h.shape
    Fd = w_up_all.shape[2]
    tm, tf = min(tm, S), min(tf, Fd)
    assert S % tm == 0 and Fd % tf == 0
    nf = Fd // tf
    n_steps = (S // tm) * nf
    final_norm = g_final is not None
    gf = (g_final if final_norm else g).reshape(1, D)
    cast_in, cast_out, cast_shapes = [], [], []
    for w_all, idx in cast_next:
        _, K, N = w_all.shape
        rows = K // n_steps
        assert rows * n_steps == K and rows % 16 == 0
        cast_in.append(pl.BlockSpec((None, rows, N), lambda i, f, idx=idx: (idx, i * nf + f, 0)))
        cast_out.append(pl.BlockSpec((None, rows, N), lambda i, f: (0, i * nf + f, 0)))
        cast_shapes.append(jax.ShapeDtypeStruct((1, K, N), BF16))
    outs = pl.pallas_call(
        functools.partial(_mlp_kernel, final_norm=final_norm, n_cast=len(cast_next)),
        grid=(S // tm, nf),
        in_specs=[pl.BlockSpec((tm, D), lambda i, f: (i, 0)),
                  pl.BlockSpec((1, D), lambda i, f: (0, 0)),
                  pl.BlockSpec((None, D, tf), lambda i, f: (layer, 0, f)),
                  pl.BlockSpec((None, tf, D), lambda i, f: (layer, f, 0)),
                  pl.BlockSpec((1, D), lambda i, f: (0, 0))] + cast_in,
        out_specs=[pl.BlockSpec((tm, D), lambda i, f: (i, 0))] + cast_out,
        out_shape=[jax.ShapeDtypeStruct((S, D), F32)] + cast_shapes,
        scratch_shapes=[pltpu.VMEM((tm, D), BF16)],
        compiler_params=_params("arbitrary", "arbitrary"),
        name="mlp_residual",
    )(h, g.reshape(1, D), w_up_all, w_down_all, gf, *[w for w, _ in cast_next])
    return outs[0], list(outs[1:])


def _moba_gate_kernel(q_ref, k_ref, qaug_ref, kmean_ref, *, nb, tq):
    blk = MOBA_BLOCK
    i = pl.program_id(1)

    @pl.when(i == 0)
    def _():
        for n in range(nb):
            kn = k_ref[n * blk:(n + 1) * blk, :].astype(F32)
            kmean_ref[n:n + 1, :] = jnp.mean(kn, axis=0, keepdims=True)

    q = q_ref[...]
    km = kmean_ref[...]
    km_hi = km.astype(BF16)
    r1 = km - km_hi.astype(F32)
    km_mid = r1.astype(BF16)
    km_lo = (r1 - km_mid.astype(F32)).astype(BF16)
    gate = _dot_nt(km_hi, q) + _dot_nt(km_mid, q) + _dot_nt(km_lo, q)

    blk_id = lax.broadcasted_iota(jnp.int32, (nb, tq), 0)
    tok = lax.broadcasted_iota(jnp.int32, (nb, tq), 1)
    qb = i * (tq // blk) + lax.shift_right_logical(tok, jnp.int32(blk.bit_length() - 1))
    gate = jnp.where(blk_id < qb, gate, NEG_INF)
    chosen = jnp.zeros((nb, tq), jnp.bool_)
    for n in range(min(MOBA_TOP_K, nb)):
        best = jnp.max(gate, axis=0, keepdims=True)
        idx = jnp.min(jnp.where(gate == best, blk_id, nb), axis=0, keepdims=True)
        hit = blk_id == idx
        chosen = chosen | (hit & (n < qb))
        gate = jnp.where(hit, -jnp.inf, gate)

    eye = (lax.broadcasted_iota(jnp.int32, (nb, LANES), 0)
           == lax.broadcasted_iota(jnp.int32, (nb, LANES), 1))
    picked = _dot_tn(jnp.where(chosen, 1.0, 0.0).astype(BF16), jnp.where(eye, 1.0, 0.0).astype(BF16))
    lane = lax.broadcasted_iota(jnp.int32, (tq, LANES), 1)
    qaug_ref[:, 0:HEAD_DIM] = q
    qaug_ref[:, HEAD_DIM:2 * HEAD_DIM] = jnp.where((picked > 0.5) | (lane >= nb), 0.0, NEG_INF).astype(BF16)


def _moba_slots(nb, chunk):
    return max(-(-p // chunk) + -(-(nb - 1 - p) // chunk) for p in range(nb // 2))


def _moba_attn_kernel(qlo_ref, qhi_ref, k_ref, v_ref, olo_ref, ohi_ref, ka_ref, va_ref, q_ref, s_ref, own_ref,
                      m_ref, acc_ref, *, chunk, nb, heads):
    blk = MOBA_BLOCK
    span = chunk * blk
    last_chunk = nb // chunk - 1
    n_slots = _moba_slots(nb, chunk)
    p = pl.program_id(1)
    hd2 = 2 * HEAD_DIM

    @pl.when(jnp.logical_and(pl.program_id(0) == 0, p == 0))
    def _():
        lane = lax.broadcasted_iota(jnp.int32, (blk, LANES), 1)
        ones_col = jnp.where(lane == 0, 1.0, 0.0).astype(BF16)
        for n in range(nb):
            rows = slice(n * blk, (n + 1) * blk)
            for g in range(heads):
                ka_ref[g, rows, HEAD_DIM:hd2] = jnp.where(lane == n, 1.0, 0.0).astype(BF16)
                va_ref[g, rows, HEAD_DIM:hd2] = ones_col

    @pl.when(p == 0)
    def _():
        for g in range(heads):
            ka_ref[g, :, 0:HEAD_DIM] = k_ref[:, g * HEAD_DIM:(g + 1) * HEAD_DIM]
            va_ref[g, :, 0:HEAD_DIM] = v_ref[:, g * HEAD_DIM:(g + 1) * HEAD_DIM]

    q_ref[0] = qlo_ref[...]
    q_ref[1] = qhi_ref[...]

    qbs = (p, nb - 1 - p)
    q_refs, o_refs = (qlo_ref, qhi_ref), (olo_ref, ohi_ref)
    log2_scale = HEAD_DIM ** -0.5 * math.log2(math.e)
    n_lo = (qbs[0] + chunk - 1) // chunk
    n_hi = (qbs[1] + chunk - 1) // chunk

    def slot(c):
        is_hi = c < n_hi
        is_lo = jnp.logical_and(c >= n_hi, c < n_hi + n_lo)
        ck = jnp.where(is_hi, c, jnp.where(is_lo, c - n_hi, last_chunk))
        return is_hi, jnp.where(is_hi, 1, 0), pl.multiple_of(ck * span, span)

    row = lax.broadcasted_iota(jnp.int32, (blk, blk), 0)
    col = lax.broadcasted_iota(jnp.int32, (blk, blk), 1)
    starts = [pl.multiple_of(qb * blk, blk) for qb in qbs]
    for t in range(2):
        for g in range(heads):
            q = q_refs[t][:, g * hd2:g * hd2 + HEAD_DIM]
            s = _dot_nt(q, ka_ref[g, pl.ds(starts[t], blk), 0:HEAD_DIM]) * log2_scale
            s = jnp.where(col <= row, s, NEG_INF)
            own_ref[g, t] = s
            m_ref[g, t] = functools.reduce(jnp.maximum, _lane_groups(s))
    for c in range(n_slots):
        is_hi, which, st = slot(c)
        for g in range(heads):
            cols = slice(g * hd2, (g + 1) * hd2)
            s = _dot_nt(q_ref[which, :, cols], ka_ref[g, pl.ds(st, span), :]) * log2_scale
            s_ref[g, c] = s
            m_ref[g, which] = functools.reduce(jnp.maximum, _lane_groups(s), m_ref[g, which])
    for t in range(2):
        for g in range(heads):
            m_ref[g, t] = jnp.broadcast_to(jnp.max(m_ref[g, t], axis=1, keepdims=True), (blk, LANES))

    for t in range(2):
        for g in range(heads):
            m = m_ref[g, t]
            pr = [jnp.exp2(s - m) for s in _lane_groups(own_ref[g, t])]
            acc_ref[g, t] = _dot(jnp.concatenate(pr, axis=1).astype(BF16), va_ref[g, pl.ds(starts[t], blk), :])
    for c in range(n_slots):
        _, which, st = slot(c)
        for g in range(heads):
            m = m_ref[g, which]
            pr = [jnp.exp2(s - m) for s in _lane_groups(s_ref[g, c])]
            acc_ref[g, which] += _dot(jnp.concatenate(pr, axis=1).astype(BF16), va_ref[g, pl.ds(st, span), :])
    for t in range(2):
        for g in range(heads):
            acc = acc_ref[g, t]
            out = acc[:, 0:HEAD_DIM] * (1.0 / acc[:, HEAD_DIM:HEAD_DIM + 1])
            o_refs[t][:, g * HEAD_DIM:(g + 1) * HEAD_DIM] = out.astype(o_refs[t].dtype)


def moba_attention(qk, v, n_heads, *, tq=4096, heads=2):
    S = qk.shape[0]
    blk, hd, H, chunk = MOBA_BLOCK, HEAD_DIM, n_heads, MOBA_CHUNK
    tq = min(tq, S)
    assert S % tq == 0 and tq % blk == 0 and hd == LANES and blk & (blk - 1) == 0
    nb = S // blk
    assert nb <= LANES and nb % 8 == 0 and nb % (2 * chunk) == 0
    half = nb // 2
    qaug = pl.pallas_call(
        functools.partial(_moba_gate_kernel, nb=nb, tq=tq),
        grid=(H, S // tq),
        in_specs=[pl.BlockSpec((tq, hd), lambda h, i: (i, h)),
                  pl.BlockSpec((S, hd), lambda h, i: (0, H + h))],
        out_specs=pl.BlockSpec((tq, 2 * hd), lambda h, i: (i, h)),
        out_shape=jax.ShapeDtypeStruct((S, H * 2 * hd), BF16),
        scratch_shapes=[pltpu.VMEM((nb, hd), F32)],
        compiler_params=_params("parallel", "arbitrary"),
        name="moba_gate",
    )(qk, qk)
    out = jax.ShapeDtypeStruct((S // 2, H * hd), BF16)
    G = heads
    assert H % G == 0
    return pl.pallas_call(
        functools.partial(_moba_attn_kernel, chunk=chunk, nb=nb, heads=G),
        grid=(H // G, half),
        in_specs=[pl.BlockSpec((blk, G * 2 * hd), lambda h, p: (p, h)),
                  pl.BlockSpec((blk, G * 2 * hd), lambda h, p: (nb - 1 - p, h)),
                  pl.BlockSpec((S, G * hd), lambda h, p: (0, H // G + h)),
                  pl.BlockSpec((S, G * hd), lambda h, p: (0, h))],
        out_specs=[pl.BlockSpec((blk, G * hd), lambda h, p: (p, h)),
                   pl.BlockSpec((blk, G * hd), lambda h, p: (half - 1 - p, h))],
        out_shape=[out, out],
        scratch_shapes=[pltpu.VMEM((G, S, 2 * hd), BF16),
                        pltpu.VMEM((G, S, 2 * hd), BF16),
                        pltpu.VMEM((2, blk, G * 2 * hd), BF16),
                        pltpu.VMEM((G, _moba_slots(nb, chunk), blk, chunk * blk), F32),
                        pltpu.VMEM((G, 2, blk, blk), F32),
                        pltpu.VMEM((G, 2, blk, LANES), F32),
                        pltpu.VMEM((G, 2, blk, 2 * hd), F32)],
        compiler_params=pltpu.CompilerParams(dimension_semantics=("arbitrary", "arbitrary"),
                                             vmem_limit_bytes=MOBA_VMEM_LIMIT),
        name="moba_attention",
    )(qaug, qaug, qk, v)


TAIL = 8


def _conv_mix_kernel(x_ref, g_ref, wb_ref, wc_ref, wu_ref, cw_ref, a_ref, xn_ref, tail_ref):
    i, j = pl.program_id(0), pl.program_id(1)

    @pl.when(j == 0)
    def _():
        xn_ref[...] = _rmsnorm_rows(x_ref[...], g_ref[...]).astype(BF16)

    xn = xn_ref[...]
    z = _dot(xn, wc_ref[...]) * _dot(xn, wu_ref[...])
    tm = z.shape[0]
    prev = jnp.where(i == 0, 0.0, tail_ref[j])
    row = lax.broadcasted_iota(jnp.int32, z.shape, 0)
    zm1 = jnp.where(row == 0, prev[TAIL - 1:TAIL, :], pltpu.roll(z, 1, 0))
    zm2 = jnp.where(row == 0, prev[TAIL - 2:TAIL - 1, :],
                    jnp.where(row == 1, prev[TAIL - 1:TAIL, :], pltpu.roll(z, 2, 0)))
    cw = cw_ref[...]
    conv = cw[0:1, :] * zm2 + cw[1:2, :] * zm1 + cw[2:3, :] * z
    a_ref[...] = (_dot(xn, wb_ref[...]) * conv).astype(a_ref.dtype)
    tail_ref[j] = z[tm - TAIL:tm, :]


def conv_mix(x, g, w_in_all, layer, conv_w, *, tm=512, tn=512):
    S, D = x.shape
    tm, tn = min(tm, S), min(tn, D)
    assert S % tm == 0 and D % tn == 0 and w_in_all.shape[2] == 3 * D and tm % TAIL == 0
    assert conv_w.shape == (CONV_WIDTH, D)
    nj = D // tn
    return pl.pallas_call(
        _conv_mix_kernel,
        grid=(S // tm, nj),
        in_specs=[pl.BlockSpec((tm, D), lambda i, j: (i, 0)),
                  pl.BlockSpec((1, D), lambda i, j: (0, 0)),
                  pl.BlockSpec((None, D, tn), lambda i, j: (layer, 0, j)),
                  pl.BlockSpec((None, D, tn), lambda i, j: (layer, 0, nj + j)),
                  pl.BlockSpec((None, D, tn), lambda i, j: (layer, 0, 2 * nj + j)),
                  pl.BlockSpec((CONV_WIDTH, tn), lambda i, j: (0, j))],
        out_specs=pl.BlockSpec((tm, tn), lambda i, j: (i, j)),
        out_shape=jax.ShapeDtypeStruct((S, D), BF16),
        scratch_shapes=[pltpu.VMEM((tm, D), BF16), pltpu.VMEM((nj, TAIL, tn), F32)],
        compiler_params=_params("arbitrary", "arbitrary"),
        name="conv_mix",
    )(x, g.reshape(1, D), w_in_all, w_in_all, w_in_all, conv_w)


def _dilated_kernel(q_ref, k_ref, v_ref, o_ref, lse_ref, kband_ref, vband_ref, *, band, n_heads):
    j = pl.program_id(1)
    scale = HEAD_DIM ** -0.5

    @pl.when(j == 0)
    def _():
        kband_ref[0:band, :] = jnp.zeros((band, kband_ref.shape[1]), BF16)
        vband_ref[0:band, :] = jnp.zeros((band, vband_ref.shape[1]), BF16)

    kband_ref[band:2 * band, :] = k_ref[...]
    vband_ref[band:2 * band, :] = v_ref[...]
    a = lax.broadcasted_iota(jnp.int32, (band, 2 * band), 0)
    b = lax.broadcasted_iota(jnp.int32, (band, 2 * band), 1)
    valid = (b >= a) & (b <= a + band) & ((j > 0) | (b >= band))
    for hh in range(n_heads):
        sl = slice(hh * HEAD_DIM, (hh + 1) * HEAD_DIM)
        s = _dot_nt(q_ref[:, sl], kband_ref[:, sl]) * scale
        s = jnp.where(valid, s, NEG_INF)
        m = jnp.max(s, axis=1, keepdims=True)
        e = jnp.exp(s - m)
        l = jnp.sum(e, axis=1, keepdims=True)
        o_ref[:, sl] = (_dot(e.astype(BF16), vband_ref[:, sl]) * (1.0 / l)).astype(o_ref.dtype)
        lse_ref[:, sl] = jnp.broadcast_to(m + jnp.log(l), (band, HEAD_DIM))
    kband_ref[0:band, :] = k_ref[...]
    vband_ref[0:band, :] = v_ref[...]


def dilated_group_attention(qk, v, window, dilation, n_heads):
    dil, L, W = qk.shape
    band = window // dilation
    width = n_heads * HEAD_DIM
    assert dil == dilation and L % band == 0 and W == 2 * width and v.shape == (dil, L, width)
    nbl = L // band

    def spec(t):
        return pl.BlockSpec((None, band, width), lambda r, j: (r, j, t))

    return pl.pallas_call(
        functools.partial(_dilated_kernel, band=band, n_heads=n_heads),
        grid=(dil, nbl),
        in_specs=[spec(0), spec(1), spec(0)],
        out_specs=[pl.BlockSpec((None, band, width), lambda r, j: (r, j, 0))] * 2,
        out_shape=[jax.ShapeDtypeStruct((dil, L, width), BF16),
                   jax.ShapeDtypeStruct((dil, L, width), F32)],
        scratch_shapes=[pltpu.VMEM((2 * band, width), BF16)] * 2,
        compiler_params=_params("parallel", "arbitrary"),
        name="dilated_attention",
    )(qk, qk, v)


def _combine_kernel(*refs, dilations):
    n = len(dilations)
    o_refs, l_refs, out_ref = refs[:n], refs[n:2 * n], refs[2 * n]
    scratch = list(refs[2 * n + 1:])
    tm, width = out_ref.shape

    views = []
    for g, dil in enumerate(dilations):
        if dil == 1:
            views.append((lambda sl, o=o_refs[g]: o[0, :, sl].astype(F32), lambda sl, l=l_refs[g]: l[0, :, sl]))
            continue
        o_scr, l_scr = scratch.pop(0), scratch.pop(0)
        rows = tm // dil
        for c in range(width // LANES):
            sl = slice(c * LANES, (c + 1) * LANES)
            for r in range(dil):
                o_scr[c, pl.ds(r, rows, stride=dil), :] = o_refs[g][r, :, sl].astype(F32)
                l_scr[c, pl.ds(r, rows, stride=dil), :] = l_refs[g][r, :, sl]
        views.append((lambda sl, s=o_scr: s[sl.start // LANES], lambda sl, s=l_scr: s[sl.start // LANES]))

    for c in range(width // LANES):
        sl = slice(c * LANES, (c + 1) * LANES)
        lses = [lv(sl) for _, lv in views]
        top = functools.reduce(jnp.maximum, lses)
        ws = [jnp.exp(l - top) for l in lses]
        inv = 1.0 / functools.reduce(jnp.add, ws)
        acc = (ws[0] * inv) * views[0][0](sl)
        for w, (ov, _) in zip(ws[1:], views[1:]):
            acc = acc + (w * inv) * ov(sl)
        out_ref[:, sl] = acc.astype(out_ref.dtype)


def combine_groups(outs, lses, dilations, *, tm=256):
    width = outs[0].shape[2]
    S = outs[0].shape[0] * outs[0].shape[1]
    tm = min(tm, S)
    assert S % tm == 0 and all(tm % (8 * d) == 0 for d in dilations)
    specs = [pl.BlockSpec((d, tm // d, width), lambda i: (0, i, 0)) for d in dilations]
    scratch = []
    for d in dilations:
        if d > 1:
            scratch += [pltpu.VMEM((width // LANES, tm, LANES), F32)] * 2
    return pl.pallas_call(
        functools.partial(_combine_kernel, dilations=tuple(dilations)),
        grid=(S // tm,),
        in_specs=specs + specs,
        out_specs=pl.BlockSpec((tm, width), lambda i: (i, 0)),
        out_shape=jax.ShapeDtypeStruct((S, width), BF16),
        scratch_shapes=scratch,
        compiler_params=_params("parallel"),
        name="combine_groups",
    )(*outs, *lses)


def kernel(x, positions, norm_mix, norm_mlp, norm_final, mlp_w_up, mlp_w_down, moba_w_qkv, moba_w_o,
           conv_w_in, conv_w, conv_w_out, dil_w_qkv, dil_w_o):
    B, S, D = x.shape
    assert B == 1 and D % HEAD_DIM == 0
    n_heads = D // HEAD_DIM
    depth = norm_mix.shape[0]
    h = x[0]
    tables = rope_tables(positions[0])
    mixers = ((moba_w_qkv, moba_w_o), (conv_w_in, conv_w_out), (dil_w_qkv, dil_w_o))

    def f32_weights(i):
        kind, j = i % N_MIXERS, i // N_MIXERS
        return [(mixers[kind][0], j), (mixers[kind][1], j), (mlp_w_up, i), (mlp_w_down, i)]

    weights = [w[idx:idx + 1].astype(BF16) for w, idx in f32_weights(0)]
    for i in range(depth):
        kind = i % N_MIXERS
        w_in, w_out, w_up, w_down = weights
        if kind == 0:
            qk, v = norm_matmul_rope(h, norm_mix[i], w_in, 0, tables)
            h = matmul_residual(moba_attention(qk[0], v[0], n_heads), w_out, 0, h)
        elif kind == 1:
            a = conv_mix(h, norm_mix[i], w_in, 0, conv_w[i // N_MIXERS])
            h = matmul_residual([a], w_out, 0, h)
        else:
            outs, lses = [], []
            for g, (window, dilation) in enumerate(DIL_GROUPS):
                qk, v = norm_matmul_rope(h, norm_mix[i], w_in, 0, tables, sec0=3 * g, dilation=dilation)
                o_g, lse_g = dilated_group_attention(qk, v, window, dilation, n_heads)
                outs.append(o_g)
                lses.append(lse_g)
            o = combine_groups(outs, lses, [d for _, d in DIL_GROUPS])
            h = matmul_residual([o], w_out, 0, h)
        last = i == depth - 1
        h, weights = mlp_residual(h, norm_mlp[i], w_up, w_down, 0, norm_final if last else None,
                                  () if last else f32_weights(i + 1))
    return h[None]
```

```python
import functools
import math

import jax
import jax.numpy as jnp
from jax import lax
from jax.experimental import pallas as pl
from jax.experimental.pallas import tpu as pltpu

HEAD_DIM = 128
MOBA_BLOCK = 256
MOBA_TOP_K = 3
MOBA_CHUNK = 2
CONV_WIDTH = 3
DIL_GROUPS = ((128, 1), (512, 4), (2048, 16))
ROT_DIM = HEAD_DIM // 4
ROPE_THETA = 500000.0
RMS_EPS = 1e-5
NEG_INF = -1e30
N_MIXERS = 3

LANES = 128
VMEM_LIMIT = 56 * 1024 * 1024
MOBA_VMEM_LIMIT = 60 * 1024 * 1024

F32 = jnp.float32
BF16 = jnp.bfloat16


def _params(*sem):
    return pltpu.CompilerParams(dimension_semantics=sem, vmem_limit_bytes=VMEM_LIMIT)


def _rmsnorm_rows(x, g):
    y = x * lax.rsqrt(jnp.mean(x * x, axis=-1, keepdims=True) + RMS_EPS)
    return y * g


def _dot(a, b):
    return jnp.dot(a, b, preferred_element_type=F32)


def _dot_nt(a, b):
    return lax.dot_general(a, b, (((1,), (1,)), ((), ())), preferred_element_type=F32)


def _dot_tn(a, b):
    return lax.dot_general(a, b, (((0,), (0,)), ((), ())), preferred_element_type=F32)


def _lane_groups(x):
    return [x[:, g * LANES:(g + 1) * LANES] for g in range(x.shape[1] // LANES)]


def _rope_table_kernel(pos_ref, invf_ref, cos_ref, sa_ref, sb_ref):
    ang = pos_ref[...] * invf_ref[...]
    lane = lax.broadcasted_iota(jnp.int32, ang.shape, 1)
    c, s = jnp.cos(ang), jnp.sin(ang)
    half = ROT_DIM // 2
    cos_ref[...] = jnp.where(lane < ROT_DIM, c, 1.0)
    sa_ref[...] = jnp.where(lane < half, -s, 0.0)
    sb_ref[...] = jnp.where((lane >= half) & (lane < ROT_DIM), s, 0.0)


def rope_tables(positions):
    S = positions.shape[0]
    tm = min(S, 1024)
    half = ROT_DIM // 2
    inv_freq = ROPE_THETA ** (-jnp.arange(0, ROT_DIM, 2, dtype=F32) / ROT_DIM)
    invf = jnp.concatenate([inv_freq, inv_freq, jnp.zeros((LANES - 2 * half,), F32)])[None, :]
    pos = positions.astype(F32)[:, None]
    tab = jax.ShapeDtypeStruct((S, LANES), F32)
    return pl.pallas_call(
        _rope_table_kernel,
        grid=(S // tm,),
        in_specs=[pl.BlockSpec((tm, 1), lambda i: (i, 0)),
                  pl.BlockSpec((1, LANES), lambda i: (0, 0))],
        out_specs=[pl.BlockSpec((tm, LANES), lambda i: (i, 0))] * 3,
        out_shape=[tab, tab, tab],
        compiler_params=_params("arbitrary"),
        name="rope_tables",
    )(pos, invf)


def _apply_rope(t, cos, sa, sb):
    half = ROT_DIM // 2
    return (t * cos + pltpu.roll(t, LANES - half, 1) * sa + pltpu.roll(t, half, 1) * sb)


def _norm_mm_rope_kernel(x_ref, g_ref, wr_ref, wv_ref, cos_ref, sa_ref, sb_ref, oqk_ref, ov_ref, xn_ref,
                         *scratch, dilation):
    @pl.when(pl.program_id(1) == 0)
    def _():
        xn_ref[...] = _rmsnorm_rows(x_ref[...], g_ref[...]).astype(BF16)

    xn = xn_ref[...]
    acc_r = _dot(xn, wr_ref[...])
    acc_v = _dot(xn, wv_ref[...])
    heads_r, heads_v = acc_r.shape[1] // HEAD_DIM, acc_v.shape[1] // HEAD_DIM
    if dilation == 1:
        cos, sa, sb = cos_ref[...], sa_ref[...], sb_ref[...]
        for c in range(heads_r):
            sl = slice(c * HEAD_DIM, (c + 1) * HEAD_DIM)
            oqk_ref[0, :, sl] = _apply_rope(acc_r[:, sl], cos, sa, sb).astype(oqk_ref.dtype)
        ov_ref[0] = acc_v.astype(ov_ref.dtype)
    else:
        acc_ref, = scratch
        for c in range(heads_r):
            acc_ref[c] = acc_r[:, c * HEAD_DIM:(c + 1) * HEAD_DIM]
        for c in range(heads_v):
            acc_ref[heads_r + c] = acc_v[:, c * HEAD_DIM:(c + 1) * HEAD_DIM]
        rows = acc_r.shape[0] // dilation
        for r in range(dilation):
            pick = pl.ds(r, rows, stride=dilation)
            cos, sa, sb = cos_ref[pick, :], sa_ref[pick, :], sb_ref[pick, :]
            for c in range(heads_r):
                sl = slice(c * HEAD_DIM, (c + 1) * HEAD_DIM)
                oqk_ref[r, :, sl] = _apply_rope(acc_ref[c, pick, :], cos, sa, sb).astype(oqk_ref.dtype)
        for r in range(dilation):
            pick = pl.ds(r, rows, stride=dilation)
            for c in range(heads_v):
                sl = slice(c * HEAD_DIM, (c + 1) * HEAD_DIM)
                ov_ref[r, :, sl] = acc_ref[heads_r + c, pick, :].astype(ov_ref.dtype)


def norm_matmul_rope(x, g, w_all, layer, tables, *, sec0=0, dilation=1, tm=512):
    S, D = x.shape
    tm = min(tm, S)
    assert S % tm == 0 and tm % (16 * dilation) == 0 and D % (2 * HEAD_DIM) == 0
    rows = tm // dilation
    half = D // 2
    scratch = [pltpu.VMEM((tm, D), BF16)]
    if dilation > 1:
        scratch.append(pltpu.VMEM(((D + half) // HEAD_DIM, tm, HEAD_DIM), F32))
    L = S // dilation
    return pl.pallas_call(
        functools.partial(_norm_mm_rope_kernel, dilation=dilation),
        grid=(S // tm, 2),
        in_specs=[pl.BlockSpec((tm, D), lambda i, j: (i, 0)),
                  pl.BlockSpec((1, D), lambda i, j: (0, 0)),
                  pl.BlockSpec((None, D, D), lambda i, j: (layer, 0, sec0 + j)),
                  pl.BlockSpec((None, D, half), lambda i, j: (layer, 0, 2 * (sec0 + 2) + j))]
                 + [pl.BlockSpec((tm, LANES), lambda i, j: (i, 0))] * 3,
        out_specs=[pl.BlockSpec((dilation, rows, D), lambda i, j: (0, i, j)),
                   pl.BlockSpec((dilation, rows, half), lambda i, j: (0, i, j))],
        out_shape=[jax.ShapeDtypeStruct((dilation, L, 2 * D), BF16),
                   jax.ShapeDtypeStruct((dilation, L, D), BF16)],
        scratch_shapes=scratch,
        compiler_params=_params("parallel", "arbitrary"),
        name="norm_matmul_rope",
    )(x, g.reshape(1, D), w_all, w_all, *tables)


def _mm_res_kernel(*refs, n_parts, tiles_per_part):
    a_refs, (w_ref, h_ref, o_ref) = refs[:n_parts], refs[n_parts:]
    part = pl.program_id(0) // tiles_per_part
    for p, a_ref in enumerate(a_refs):
        @pl.when(part == p)
        def _(a_ref=a_ref):
            o_ref[...] = h_ref[...] + _dot(a_ref[...], w_ref[...])


def matmul_residual(a_parts, w_all, layer, h, *, tm=512, tn=2048):
    n_parts = len(a_parts)
    Sp, K = a_parts[0].shape
    S, N = h.shape
    tm, tn = min(tm, Sp), min(tn, N)
    assert Sp * n_parts == S and Sp % tm == 0 and N % tn == 0
    tiles = Sp // tm

    def a_spec(p):
        return pl.BlockSpec((tm, K), lambda i, j: (jnp.clip(i - p * tiles, 0, tiles - 1), 0))

    return pl.pallas_call(
        functools.partial(_mm_res_kernel, n_parts=n_parts, tiles_per_part=tiles),
        grid=(S // tm, N // tn),
        in_specs=[a_spec(p) for p in range(n_parts)]
                 + [pl.BlockSpec((None, K, tn), lambda i, j: (layer, 0, j)),
                    pl.BlockSpec((tm, tn), lambda i, j: (i, j))],
        out_specs=pl.BlockSpec((tm, tn), lambda i, j: (i, j)),
        out_shape=jax.ShapeDtypeStruct((S, N), F32),
        compiler_params=_params("parallel", "arbitrary"),
        name="matmul_residual",
    )(*a_parts, w_all, h)


def _cast_specs(jobs, n_steps, step_of):
    ins, outs, shapes = [], [], []
    for w_all, idx in jobs:
        _, K, N = w_all.shape
        rows = K // n_steps
        assert rows * n_steps == K and rows % 16 == 0
        ins.append(pl.BlockSpec((None, rows, N), lambda i, j, idx=idx: (idx, step_of(i, j), 0)))
        outs.append(pl.BlockSpec((None, rows, N), lambda i, j: (0, step_of(i, j), 0)))
        shapes.append(jax.ShapeDtypeStruct((1, K, N), BF16))
    return ins, outs, shapes


def _mlp_kernel(*refs, final_norm, n_cast):
    h_ref, g_ref, wu_ref, wd_ref, gf_ref = refs[:5]
    src_refs = refs[5:5 + n_cast]
    o_ref = refs[5 + n_cast]
    dst_refs = refs[6 + n_cast:6 + 2 * n_cast]
    xn_ref = refs[6 + 2 * n_cast]
    f = pl.program_id(1)

    @pl.when(f == 0)
    def _():
        h = h_ref[...]
        xn_ref[...] = _rmsnorm_rows(h, g_ref[...]).astype(BF16)
        o_ref[...] = h

    hid = jnp.maximum(_dot(xn_ref[...], wu_ref[...]), 0.0)
    o_ref[...] += _dot((hid * hid).astype(BF16), wd_ref[...])

    if final_norm:
        @pl.when(f == pl.num_programs(1) - 1)
        def _():
            o_ref[...] = _rmsnorm_rows(o_ref[...], gf_ref[...])

    for src_ref, dst_ref in zip(src_refs, dst_refs):
        dst_ref[...] = src_ref[...].astype(dst_ref.dtype)


def mlp_residual(h, g, w_up_all, w_down_all, layer, g_final=None, cast_next=(), *, tm=512, tf=1024):
    S, D = h.shape
    Fd = w_up_all.shape[2]
    tm, tf = min(tm, S), min(tf, Fd)
    assert S % tm == 0 and Fd % tf == 0
    nf = Fd // tf
    n_steps = (S // tm) * nf
    final_norm = g_final is not None
    gf = (g_final if final_norm else g).reshape(1, D)
    cast_in, cast_out, cast_shapes = _cast_specs(cast_next, n_steps, lambda i, f: i * nf + f)
    outs = pl.pallas_call(
        functools.partial(_mlp_kernel, final_norm=final_norm, n_cast=len(cast_next)),
        grid=(S // tm, nf),
        in_specs=[pl.BlockSpec((tm, D), lambda i, f: (i, 0)),
                  pl.BlockSpec((1, D), lambda i, f: (0, 0)),
                  pl.BlockSpec((None, D, tf), lambda i, f: (layer, 0, f)),
                  pl.BlockSpec((None, tf, D), lambda i, f: (layer, f, 0)),
                  pl.BlockSpec((1, D), lambda i, f: (0, 0))] + cast_in,
        out_specs=[pl.BlockSpec((tm, D), lambda i, f: (i, 0))] + cast_out,
        out_shape=[jax.ShapeDtypeStruct((S, D), F32)] + cast_shapes,
        scratch_shapes=[pltpu.VMEM((tm, D), BF16)],
        compiler_params=_params("arbitrary", "arbitrary"),
        name="mlp_residual",
    )(h, g.reshape(1, D), w_up_all, w_down_all, gf, *[w for w, _ in cast_next])
    return outs[0], list(outs[1:])


def _moba_gate_kernel(*refs, nb, tq, n_cast):
    q_ref, k_ref = refs[:2]
    src_refs = refs[2:2 + n_cast]
    qaug_ref = refs[2 + n_cast]
    dst_refs = refs[3 + n_cast:3 + 2 * n_cast]
    kmean_ref = refs[3 + 2 * n_cast]
    blk = MOBA_BLOCK
    i = pl.program_id(1)

    for src_ref, dst_ref in zip(src_refs, dst_refs):
        dst_ref[...] = src_ref[...].astype(dst_ref.dtype)

    @pl.when(i == 0)
    def _():
        for n in range(nb):
            kn = k_ref[n * blk:(n + 1) * blk, :].astype(F32)
            kmean_ref[n:n + 1, :] = jnp.mean(kn, axis=0, keepdims=True)

    q = q_ref[...]
    km = kmean_ref[...]
    km_hi = km.astype(BF16)
    r1 = km - km_hi.astype(F32)
    km_mid = r1.astype(BF16)
    km_lo = (r1 - km_mid.astype(F32)).astype(BF16)
    gate = _dot_nt(km_hi, q) + _dot_nt(km_mid, q) + _dot_nt(km_lo, q)

    blk_id = lax.broadcasted_iota(jnp.int32, (nb, tq), 0)
    tok = lax.broadcasted_iota(jnp.int32, (nb, tq), 1)
    qb = i * (tq // blk) + lax.shift_right_logical(tok, jnp.int32(blk.bit_length() - 1))
    gate = jnp.where(blk_id < qb, gate, NEG_INF)
    chosen = jnp.zeros((nb, tq), jnp.bool_)
    for n in range(min(MOBA_TOP_K, nb)):
        best = jnp.max(gate, axis=0, keepdims=True)
        idx = jnp.min(jnp.where(gate == best, blk_id, nb), axis=0, keepdims=True)
        hit = blk_id == idx
        chosen = chosen | (hit & (n < qb))
        gate = jnp.where(hit, -jnp.inf, gate)

    eye = (lax.broadcasted_iota(jnp.int32, (nb, LANES), 0)
           == lax.broadcasted_iota(jnp.int32, (nb, LANES), 1))
    picked = _dot_tn(jnp.where(chosen, 1.0, 0.0).astype(BF16), jnp.where(eye, 1.0, 0.0).astype(BF16))
    lane = lax.broadcasted_iota(jnp.int32, (tq, LANES), 1)
    qaug_ref[:, 0:HEAD_DIM] = q
    qaug_ref[:, HEAD_DIM:2 * HEAD_DIM] = jnp.where((picked > 0.5) | (lane >= nb), 0.0, NEG_INF).astype(BF16)


def _moba_slots(nb, chunk):
    return max(-(-p // chunk) + -(-(nb - 1 - p) // chunk) for p in range(nb // 2))


def _moba_attn_kernel(qlo_ref, qhi_ref, k_ref, v_ref, olo_ref, ohi_ref, ka_ref, va_ref, q_ref, s_ref, own_ref,
                      m_ref, acc_ref, *, chunk, nb, heads):
    blk = MOBA_BLOCK
    span = chunk * blk
    last_chunk = nb // chunk - 1
    n_slots = _moba_slots(nb, chunk)
    p = pl.program_id(1)
    hd2 = 2 * HEAD_DIM

    @pl.when(jnp.logical_and(pl.program_id(0) == 0, p == 0))
    def _():
        lane = lax.broadcasted_iota(jnp.int32, (blk, LANES), 1)
        ones_col = jnp.where(lane == 0, 1.0, 0.0).astype(BF16)
        for n in range(nb):
            rows = slice(n * blk, (n + 1) * blk)
            for g in range(heads):
                ka_ref[g, rows, HEAD_DIM:hd2] = jnp.where(lane == n, 1.0, 0.0).astype(BF16)
                va_ref[g, rows, HEAD_DIM:hd2] = ones_col

    @pl.when(p == 0)
    def _():
        for g in range(heads):
            ka_ref[g, :, 0:HEAD_DIM] = k_ref[:, g * HEAD_DIM:(g + 1) * HEAD_DIM]
            va_ref[g, :, 0:HEAD_DIM] = v_ref[:, g * HEAD_DIM:(g + 1) * HEAD_DIM]

    q_ref[0] = qlo_ref[...]
    q_ref[1] = qhi_ref[...]

    qbs = (p, nb - 1 - p)
    q_refs, o_refs = (qlo_ref, qhi_ref), (olo_ref, ohi_ref)
    log2_scale = HEAD_DIM ** -0.5 * math.log2(math.e)
    n_lo = (qbs[0] + chunk - 1) // chunk
    n_hi = (qbs[1] + chunk - 1) // chunk

    def slot(c):
        is_hi = c < n_hi
        is_lo = jnp.logical_and(c >= n_hi, c < n_hi + n_lo)
        ck = jnp.where(is_hi, c, jnp.where(is_lo, c - n_hi, last_chunk))
        return is_hi, jnp.where(is_hi, 1, 0), pl.multiple_of(ck * span, span)

    row = lax.broadcasted_iota(jnp.int32, (blk, blk), 0)
    col = lax.broadcasted_iota(jnp.int32, (blk, blk), 1)
    starts = [pl.multiple_of(qb * blk, blk) for qb in qbs]
    for t in range(2):
        for g in range(heads):
            q = q_refs[t][:, g * hd2:g * hd2 + HEAD_DIM]
            s = _dot_nt(q, ka_ref[g, pl.ds(starts[t], blk), 0:HEAD_DIM]) * log2_scale
            s = jnp.where(col <= row, s, NEG_INF)
            own_ref[g, t] = s
            m_ref[g, t] = functools.reduce(jnp.maximum, _lane_groups(s))
    for c in range(n_slots):
        is_hi, which, st = slot(c)
        for g in range(heads):
            cols = slice(g * hd2, (g + 1) * hd2)
            s = _dot_nt(q_ref[which, :, cols], ka_ref[g, pl.ds(st, span), :]) * log2_scale
            s_ref[g, c] = s
            m_ref[g, which] = functools.reduce(jnp.maximum, _lane_groups(s), m_ref[g, which])
    for t in range(2):
        for g in range(heads):
            m_ref[g, t] = jnp.broadcast_to(jnp.max(m_ref[g, t], axis=1, keepdims=True), (blk, LANES))

    for t in range(2):
        for g in range(heads):
            m = m_ref[g, t]
            pr = [jnp.exp2(s - m) for s in _lane_groups(own_ref[g, t])]
            acc_ref[g, t] = _dot(jnp.concatenate(pr, axis=1).astype(BF16), va_ref[g, pl.ds(starts[t], blk), :])
    for c in range(n_slots):
        _, which, st = slot(c)
        for g in range(heads):
            m = m_ref[g, which]
            pr = [jnp.exp2(s - m) for s in _lane_groups(s_ref[g, c])]
            acc_ref[g, which] += _dot(jnp.concatenate(pr, axis=1).astype(BF16), va_ref[g, pl.ds(st, span), :])
    for t in range(2):
        for g in range(heads):
            acc = acc_ref[g, t]
            out = acc[:, 0:HEAD_DIM] * (1.0 / acc[:, HEAD_DIM:HEAD_DIM + 1])
            o_refs[t][:, g * HEAD_DIM:(g + 1) * HEAD_DIM] = out.astype(o_refs[t].dtype)


def moba_attention(qk, v, n_heads, cast=(), *, tq=4096, heads=2):
    S = qk.shape[0]
    blk, hd, H, chunk = MOBA_BLOCK, HEAD_DIM, n_heads, MOBA_CHUNK
    tq = min(tq, S)
    assert S % tq == 0 and tq % blk == 0 and hd == LANES and blk & (blk - 1) == 0
    nb = S // blk
    assert nb <= LANES and nb % 8 == 0 and nb % (2 * chunk) == 0
    half = nb // 2
    nq = S // tq
    cast_in, cast_out, cast_shapes = _cast_specs(cast, H * nq, lambda h, i: h * nq + i)
    qaug, *casts = pl.pallas_call(
        functools.partial(_moba_gate_kernel, nb=nb, tq=tq, n_cast=len(cast)),
        grid=(H, nq),
        in_specs=[pl.BlockSpec((tq, hd), lambda h, i: (i, h)),
                  pl.BlockSpec((S, hd), lambda h, i: (0, H + h))] + cast_in,
        out_specs=[pl.BlockSpec((tq, 2 * hd), lambda h, i: (i, h))] + cast_out,
        out_shape=[jax.ShapeDtypeStruct((S, H * 2 * hd), BF16)] + cast_shapes,
        scratch_shapes=[pltpu.VMEM((nb, hd), F32)],
        compiler_params=_params("arbitrary", "arbitrary"),
        name="moba_gate",
    )(qk, qk, *[w for w, _ in cast])
    out = jax.ShapeDtypeStruct((S // 2, H * hd), BF16)
    G = heads
    assert H % G == 0
    halves = pl.pallas_call(
        functools.partial(_moba_attn_kernel, chunk=chunk, nb=nb, heads=G),
        grid=(H // G, half),
        in_specs=[pl.BlockSpec((blk, G * 2 * hd), lambda h, p: (p, h)),
                  pl.BlockSpec((blk, G * 2 * hd), lambda h, p: (nb - 1 - p, h)),
                  pl.BlockSpec((S, G * hd), lambda h, p: (0, H // G + h)),
                  pl.BlockSpec((S, G * hd), lambda h, p: (0, h))],
        out_specs=[pl.BlockSpec((blk, G * hd), lambda h, p: (p, h)),
                   pl.BlockSpec((blk, G * hd), lambda h, p: (half - 1 - p, h))],
        out_shape=[out, out],
        scratch_shapes=[pltpu.VMEM((G, S, 2 * hd), BF16),
                        pltpu.VMEM((G, S, 2 * hd), BF16),
                        pltpu.VMEM((2, blk, G * 2 * hd), BF16),
                        pltpu.VMEM((G, _moba_slots(nb, chunk), blk, chunk * blk), F32),
                        pltpu.VMEM((G, 2, blk, blk), F32),
                        pltpu.VMEM((G, 2, blk, LANES), F32),
                        pltpu.VMEM((G, 2, blk, 2 * hd), F32)],
        compiler_params=pltpu.CompilerParams(dimension_semantics=("arbitrary", "arbitrary"),
                                             vmem_limit_bytes=MOBA_VMEM_LIMIT),
        name="moba_attention",
    )(qaug, qaug, qk, v)
    return halves, casts


TAIL = 8


def _conv_mix_kernel(x_ref, g_ref, wb_ref, wc_ref, wu_ref, cw_ref, a_ref, xn_ref, tail_ref):
    i, j = pl.program_id(0), pl.program_id(1)

    @pl.when(j == 0)
    def _():
        xn_ref[...] = _rmsnorm_rows(x_ref[...], g_ref[...]).astype(BF16)

    xn = xn_ref[...]
    z = _dot(xn, wc_ref[...]) * _dot(xn, wu_ref[...])
    tm = z.shape[0]
    prev = jnp.where(i == 0, 0.0, tail_ref[j])
    row = lax.broadcasted_iota(jnp.int32, z.shape, 0)
    zm1 = jnp.where(row == 0, prev[TAIL - 1:TAIL, :], pltpu.roll(z, 1, 0))
    zm2 = jnp.where(row == 0, prev[TAIL - 2:TAIL - 1, :],
                    jnp.where(row == 1, prev[TAIL - 1:TAIL, :], pltpu.roll(z, 2, 0)))
    cw = cw_ref[...]
    conv = cw[0:1, :] * zm2 + cw[1:2, :] * zm1 + cw[2:3, :] * z
    a_ref[...] = (_dot(xn, wb_ref[...]) * conv).astype(a_ref.dtype)
    tail_ref[j] = z[tm - TAIL:tm, :]


def conv_mix(x, g, w_in_all, layer, conv_w, *, tm=512, tn=512):
    S, D = x.shape
    tm, tn = min(tm, S), min(tn, D)
    assert S % tm == 0 and D % tn == 0 and w_in_all.shape[2] == 3 * D and tm % TAIL == 0
    assert conv_w.shape == (CONV_WIDTH, D)
    nj = D // tn
    return pl.pallas_call(
        _conv_mix_kernel,
        grid=(S // tm, nj),
        in_specs=[pl.BlockSpec((tm, D), lambda i, j: (i, 0)),
                  pl.BlockSpec((1, D), lambda i, j: (0, 0)),
                  pl.BlockSpec((None, D, tn), lambda i, j: (layer, 0, j)),
                  pl.BlockSpec((None, D, tn), lambda i, j: (layer, 0, nj + j)),
                  pl.BlockSpec((None, D, tn), lambda i, j: (layer, 0, 2 * nj + j)),
                  pl.BlockSpec((CONV_WIDTH, tn), lambda i, j: (0, j))],
        out_specs=pl.BlockSpec((tm, tn), lambda i, j: (i, j)),
        out_shape=jax.ShapeDtypeStruct((S, D), BF16),
        scratch_shapes=[pltpu.VMEM((tm, D), BF16), pltpu.VMEM((nj, TAIL, tn), F32)],
        compiler_params=_params("arbitrary", "arbitrary"),
        name="conv_mix",
    )(x, g.reshape(1, D), w_in_all, w_in_all, w_in_all, conv_w)


def _dilated_kernel(q_ref, k_ref, v_ref, o_ref, lse_ref, kband_ref, vband_ref, *, band, n_heads):
    j = pl.program_id(1)
    scale = HEAD_DIM ** -0.5

    @pl.when(j == 0)
    def _():
        kband_ref[0:band, :] = jnp.zeros((band, kband_ref.shape[1]), BF16)
        vband_ref[0:band, :] = jnp.zeros((band, vband_ref.shape[1]), BF16)

    kband_ref[band:2 * band, :] = k_ref[...]
    vband_ref[band:2 * band, :] = v_ref[...]
    a = lax.broadcasted_iota(jnp.int32, (band, 2 * band), 0)
    b = lax.broadcasted_iota(jnp.int32, (band, 2 * band), 1)
    valid = (b >= a) & (b <= a + band) & ((j > 0) | (b >= band))
    for hh in range(n_heads):
        sl = slice(hh * HEAD_DIM, (hh + 1) * HEAD_DIM)
        s = _dot_nt(q_ref[:, sl], kband_ref[:, sl]) * scale
        s = jnp.where(valid, s, NEG_INF)
        m = jnp.max(s, axis=1, keepdims=True)
        e = jnp.exp(s - m)
        l = jnp.sum(e, axis=1, keepdims=True)
        o_ref[:, sl] = (_dot(e.astype(BF16), vband_ref[:, sl]) * (1.0 / l)).astype(o_ref.dtype)
        lse_ref[:, sl] = jnp.broadcast_to(m + jnp.log(l), (band, HEAD_DIM))
    kband_ref[0:band, :] = k_ref[...]
    vband_ref[0:band, :] = v_ref[...]


def dilated_group_attention(qk, v, window, dilation, n_heads):
    dil, L, W = qk.shape
    band = window // dilation
    width = n_heads * HEAD_DIM
    assert dil == dilation and L % band == 0 and W == 2 * width and v.shape == (dil, L, width)
    nbl = L // band

    def spec(t):
        return pl.BlockSpec((None, band, width), lambda r, j: (r, j, t))

    return pl.pallas_call(
        functools.partial(_dilated_kernel, band=band, n_heads=n_heads),
        grid=(dil, nbl),
        in_specs=[spec(0), spec(1), spec(0)],
        out_specs=[pl.BlockSpec((None, band, width), lambda r, j: (r, j, 0))] * 2,
        out_shape=[jax.ShapeDtypeStruct((dil, L, width), BF16),
                   jax.ShapeDtypeStruct((dil, L, width), F32)],
        scratch_shapes=[pltpu.VMEM((2 * band, width), BF16)] * 2,
        compiler_params=_params("parallel", "arbitrary"),
        name="dilated_attention",
    )(qk, qk, v)


def _combine_kernel(*refs, dilations):
    n = len(dilations)
    o_refs, l_refs, out_ref = refs[:n], refs[n:2 * n], refs[2 * n]
    scratch = list(refs[2 * n + 1:])
    tm, width = out_ref.shape

    views = []
    for g, dil in enumerate(dilations):
        if dil == 1:
            views.append((lambda sl, o=o_refs[g]: o[0, :, sl].astype(F32), lambda sl, l=l_refs[g]: l[0, :, sl]))
            continue
        o_scr, l_scr = scratch.pop(0), scratch.pop(0)
        rows = tm // dil
        for c in range(width // LANES):
            sl = slice(c * LANES, (c + 1) * LANES)
            for r in range(dil):
                o_scr[c, pl.ds(r, rows, stride=dil), :] = o_refs[g][r, :, sl].astype(F32)
                l_scr[c, pl.ds(r, rows, stride=dil), :] = l_refs[g][r, :, sl]
        views.append((lambda sl, s=o_scr: s[sl.start // LANES], lambda sl, s=l_scr: s[sl.start // LANES]))

    for c in range(width // LANES):
        sl = slice(c * LANES, (c + 1) * LANES)
        lses = [lv(sl) for _, lv in views]
        top = functools.reduce(jnp.maximum, lses)
        ws = [jnp.exp(l - top) for l in lses]
        inv = 1.0 / functools.reduce(jnp.add, ws)
        acc = (ws[0] * inv) * views[0][0](sl)
        for w, (ov, _) in zip(ws[1:], views[1:]):
            acc = acc + (w * inv) * ov(sl)
        out_ref[:, sl] = acc.astype(out_ref.dtype)


def combine_groups(outs, lses, dilations, *, tm=256):
    width = outs[0].shape[2]
    S = outs[0].shape[0] * outs[0].shape[1]
    tm = min(tm, S)
    assert S % tm == 0 and all(tm % (8 * d) == 0 for d in dilations)
    specs = [pl.BlockSpec((d, tm // d, width), lambda i: (0, i, 0)) for d in dilations]
    scratch = []
    for d in dilations:
        if d > 1:
            scratch += [pltpu.VMEM((width // LANES, tm, LANES), F32)] * 2
    return pl.pallas_call(
        functools.partial(_combine_kernel, dilations=tuple(dilations)),
        grid=(S // tm,),
        in_specs=specs + specs,
        out_specs=pl.BlockSpec((tm, width), lambda i: (i, 0)),
        out_shape=jax.ShapeDtypeStruct((S, width), BF16),
        scratch_shapes=scratch,
        compiler_params=_params("parallel"),
        name="combine_groups",
    )(*outs, *lses)


def kernel(x, positions, norm_mix, norm_mlp, norm_final, mlp_w_up, mlp_w_down, moba_w_qkv, moba_w_o,
           conv_w_in, conv_w, conv_w_out, dil_w_qkv, dil_w_o):
    B, S, D = x.shape
    assert B == 1 and D % HEAD_DIM == 0
    n_heads = D // HEAD_DIM
    depth = norm_mix.shape[0]
    h = x[0]
    tables = rope_tables(positions[0])
    mixers = ((moba_w_qkv, moba_w_o), (conv_w_in, conv_w_out), (dil_w_qkv, dil_w_o))

    def f32_weights(i):
        kind, j = i % N_MIXERS, i // N_MIXERS
        return [(mixers[kind][0], j), (mixers[kind][1], j), (mlp_w_up, i), (mlp_w_down, i)]

    (w0, idx0), *late0 = f32_weights(0)
    weights = [w0[idx0:idx0 + 1].astype(BF16), None, None, None]
    for i in range(depth):
        kind = i % N_MIXERS
        w_in, w_out, w_up, w_down = weights
        if kind == 0:
            qk, v = norm_matmul_rope(h, norm_mix[i], w_in, 0, tables)
            halves, casts = moba_attention(qk[0], v[0], n_heads, late0 if i == 0 else ())
            if i == 0:
                w_out, w_up, w_down = casts
            h = matmul_residual(halves, w_out, 0, h)
        elif kind == 1:
            a = conv_mix(h, norm_mix[i], w_in, 0, conv_w[i // N_MIXERS])
            h = matmul_residual([a], w_out, 0, h)
        else:
            outs, lses = [], []
            for g, (window, dilation) in enumerate(DIL_GROUPS):
                qk, v = norm_matmul_rope(h, norm_mix[i], w_in, 0, tables, sec0=3 * g, dilation=dilation)
                o_g, lse_g = dilated_group_attention(qk, v, window, dilation, n_heads)
                outs.append(o_g)
                lses.append(lse_g)
            o = combine_groups(outs, lses, [d for _, d in DIL_GROUPS])
            h = matmul_residual([o], w_out, 0, h)
        last = i == depth - 1
        h, weights = mlp_residual(h, norm_mlp[i], w_up, w_down, 0, norm_final if last else None,
                                  () if last else f32_weights(i + 1))
    return h[None]
```

```python
import functools
import math

import jax
import jax.numpy as jnp
from jax import lax
from jax.experimental import pallas as pl
from jax.experimental.pallas import tpu as pltpu

HEAD_DIM = 128
MOBA_BLOCK = 256
MOBA_TOP_K = 3
MOBA_CHUNK = 2
CONV_WIDTH = 3
DIL_GROUPS = ((128, 1), (512, 4), (2048, 16))
ROT_DIM = HEAD_DIM // 4
ROPE_THETA = 500000.0
RMS_EPS = 1e-5
NEG_INF = -1e30
N_MIXERS = 3

LANES = 128
VMEM_LIMIT = 56 * 1024 * 1024
MOBA_VMEM_LIMIT = 60 * 1024 * 1024

F32 = jnp.float32
BF16 = jnp.bfloat16


def _params(*sem):
    return pltpu.CompilerParams(dimension_semantics=sem, vmem_limit_bytes=VMEM_LIMIT)


def _rmsnorm_rows(x, g):
    y = x * lax.rsqrt(jnp.mean(x * x, axis=-1, keepdims=True) + RMS_EPS)
    return y * g


def _dot(a, b):
    return jnp.dot(a, b, preferred_element_type=F32)


def _dot_nt(a, b):
    return lax.dot_general(a, b, (((1,), (1,)), ((), ())), preferred_element_type=F32)


def _dot_tn(a, b):
    return lax.dot_general(a, b, (((0,), (0,)), ((), ())), preferred_element_type=F32)


def _lane_groups(x):
    return [x[:, g * LANES:(g + 1) * LANES] for g in range(x.shape[1] // LANES)]


def _rope_table_kernel(pos_ref, invf_ref, cos_ref, sa_ref, sb_ref):
    ang = pos_ref[...] * invf_ref[...]
    lane = lax.broadcasted_iota(jnp.int32, ang.shape, 1)
    c, s = jnp.cos(ang), jnp.sin(ang)
    half = ROT_DIM // 2
    cos_ref[...] = jnp.where(lane < ROT_DIM, c, 1.0)
    sa_ref[...] = jnp.where(lane < half, -s, 0.0)
    sb_ref[...] = jnp.where((lane >= half) & (lane < ROT_DIM), s, 0.0)


def rope_tables(positions):
    S = positions.shape[0]
    tm = min(S, 1024)
    half = ROT_DIM // 2
    inv_freq = ROPE_THETA ** (-jnp.arange(0, ROT_DIM, 2, dtype=F32) / ROT_DIM)
    invf = jnp.concatenate([inv_freq, inv_freq, jnp.zeros((LANES - 2 * half,), F32)])[None, :]
    pos = positions.astype(F32)[:, None]
    tab = jax.ShapeDtypeStruct((S, LANES), F32)
    return pl.pallas_call(
        _rope_table_kernel,
        grid=(S // tm,),
        in_specs=[pl.BlockSpec((tm, 1), lambda i: (i, 0)),
                  pl.BlockSpec((1, LANES), lambda i: (0, 0))],
        out_specs=[pl.BlockSpec((tm, LANES), lambda i: (i, 0))] * 3,
        out_shape=[tab, tab, tab],
        compiler_params=_params("arbitrary"),
        name="rope_tables",
    )(pos, invf)


def _apply_rope(t, cos, sa, sb):
    half = ROT_DIM // 2
    return (t * cos + pltpu.roll(t, LANES - half, 1) * sa + pltpu.roll(t, half, 1) * sb)


def _norm_mm_rope_kernel(x_ref, g_ref, wr_ref, wv_ref, cos_ref, sa_ref, sb_ref, oqk_ref, ov_ref, xn_ref,
                         *scratch, dilation):
    @pl.when(pl.program_id(1) == 0)
    def _():
        xn_ref[...] = _rmsnorm_rows(x_ref[...], g_ref[...]).astype(BF16)

    xn = xn_ref[...]
    acc_r = _dot(xn, wr_ref[...])
    acc_v = _dot(xn, wv_ref[...])
    heads_r, heads_v = acc_r.shape[1] // HEAD_DIM, acc_v.shape[1] // HEAD_DIM
    if dilation == 1:
        cos, sa, sb = cos_ref[...], sa_ref[...], sb_ref[...]
        for c in range(heads_r):
            sl = slice(c * HEAD_DIM, (c + 1) * HEAD_DIM)
            oqk_ref[0, :, sl] = _apply_rope(acc_r[:, sl], cos, sa, sb).astype(oqk_ref.dtype)
        ov_ref[0] = acc_v.astype(ov_ref.dtype)
    else:
        acc_ref, = scratch
        for c in range(heads_r):
            acc_ref[c] = acc_r[:, c * HEAD_DIM:(c + 1) * HEAD_DIM]
        for c in range(heads_v):
            acc_ref[heads_r + c] = acc_v[:, c * HEAD_DIM:(c + 1) * HEAD_DIM]
        rows = acc_r.shape[0] // dilation
        for r in range(dilation):
            pick = pl.ds(r, rows, stride=dilation)
            cos, sa, sb = cos_ref[pick, :], sa_ref[pick, :], sb_ref[pick, :]
            for c in range(heads_r):
                sl = slice(c * HEAD_DIM, (c + 1) * HEAD_DIM)
                oqk_ref[r, :, sl] = _apply_rope(acc_ref[c, pick, :], cos, sa, sb).astype(oqk_ref.dtype)
        for r in range(dilation):
            pick = pl.ds(r, rows, stride=dilation)
            for c in range(heads_v):
                sl = slice(c * HEAD_DIM, (c + 1) * HEAD_DIM)
                ov_ref[r, :, sl] = acc_ref[heads_r + c, pick, :].astype(ov_ref.dtype)


def norm_matmul_rope(x, g, w_all, layer, tables, *, sec0=0, dilation=1, tm=512):
    S, D = x.shape
    tm = min(tm, S)
    assert S % tm == 0 and tm % (16 * dilation) == 0 and D % (2 * HEAD_DIM) == 0
    rows = tm // dilation
    half = D // 2
    scratch = [pltpu.VMEM((tm, D), BF16)]
    if dilation > 1:
        scratch.append(pltpu.VMEM(((D + half) // HEAD_DIM, tm, HEAD_DIM), F32))
    L = S // dilation
    return pl.pallas_call(
        functools.partial(_norm_mm_rope_kernel, dilation=dilation),
        grid=(S // tm, 2),
        in_specs=[pl.BlockSpec((tm, D), lambda i, j: (i, 0)),
                  pl.BlockSpec((1, D), lambda i, j: (0, 0)),
                  pl.BlockSpec((None, D, D), lambda i, j: (layer, 0, sec0 + j)),
                  pl.BlockSpec((None, D, half), lambda i, j: (layer, 0, 2 * (sec0 + 2) + j))]
                 + [pl.BlockSpec((tm, LANES), lambda i, j: (i, 0))] * 3,
        out_specs=[pl.BlockSpec((dilation, rows, D), lambda i, j: (0, i, j)),
                   pl.BlockSpec((dilation, rows, half), lambda i, j: (0, i, j))],
        out_shape=[jax.ShapeDtypeStruct((dilation, L, 2 * D), BF16),
                   jax.ShapeDtypeStruct((dilation, L, D), BF16)],
        scratch_shapes=scratch,
        compiler_params=_params("parallel", "arbitrary"),
        name="norm_matmul_rope",
    )(x, g.reshape(1, D), w_all, w_all, *tables)


def _mm_res_kernel(*refs, n_parts, tiles_per_part):
    a_refs, (w_ref, h_ref, o_ref) = refs[:n_parts], refs[n_parts:]
    part = pl.program_id(0) // tiles_per_part
    for p, a_ref in enumerate(a_refs):
        @pl.when(part == p)
        def _(a_ref=a_ref):
            o_ref[...] = h_ref[...] + _dot(a_ref[...], w_ref[...])


def matmul_residual(a_parts, w_all, layer, h, *, tm=512, tn=2048):
    n_parts = len(a_parts)
    Sp, K = a_parts[0].shape
    S, N = h.shape
    tm, tn = min(tm, Sp), min(tn, N)
    assert Sp * n_parts == S and Sp % tm == 0 and N % tn == 0
    tiles = Sp // tm

    def a_spec(p):
        return pl.BlockSpec((tm, K), lambda i, j: (jnp.clip(i - p * tiles, 0, tiles - 1), 0))

    return pl.pallas_call(
        functools.partial(_mm_res_kernel, n_parts=n_parts, tiles_per_part=tiles),
        grid=(S // tm, N // tn),
        in_specs=[a_spec(p) for p in range(n_parts)]
                 + [pl.BlockSpec((None, K, tn), lambda i, j: (layer, 0, j)),
                    pl.BlockSpec((tm, tn), lambda i, j: (i, j))],
        out_specs=pl.BlockSpec((tm, tn), lambda i, j: (i, j)),
        out_shape=jax.ShapeDtypeStruct((S, N), F32),
        compiler_params=_params("parallel", "arbitrary"),
        name="matmul_residual",
    )(*a_parts, w_all, h)


def _cast_specs(jobs, n_steps, step_of):
    ins, outs, shapes = [], [], []
    for w_all, idx in jobs:
        _, K, N = w_all.shape
        rows = K // n_steps
        assert rows * n_steps == K and rows % 16 == 0
        ins.append(pl.BlockSpec((None, rows, N), lambda i, j, idx=idx: (idx, step_of(i, j), 0)))
        outs.append(pl.BlockSpec((None, rows, N), lambda i, j: (0, step_of(i, j), 0)))
        shapes.append(jax.ShapeDtypeStruct((1, K, N), BF16))
    return ins, outs, shapes


def _mlp_kernel(*refs, final_norm, n_cast):
    h_ref, g_ref, wu_ref, wd_ref, gf_ref = refs[:5]
    src_refs = refs[5:5 + n_cast]
    o_ref = refs[5 + n_cast]
    dst_refs = refs[6 + n_cast:6 + 2 * n_cast]
    xn_ref = refs[6 + 2 * n_cast]
    f = pl.program_id(1)

    @pl.when(f == 0)
    def _():
        h = h_ref[...]
        xn_ref[...] = _rmsnorm_rows(h, g_ref[...]).astype(BF16)
        o_ref[...] = h

    hid = jnp.maximum(_dot(xn_ref[...], wu_ref[...]), 0.0)
    o_ref[...] += _dot((hid * hid).astype(BF16), wd_ref[...])

    if final_norm:
        @pl.when(f == pl.num_programs(1) - 1)
        def _():
            o_ref[...] = _rmsnorm_rows(o_ref[...], gf_ref[...])

    for src_ref, dst_ref in zip(src_refs, dst_refs):
        dst_ref[...] = src_ref[...].astype(dst_ref.dtype)


def mlp_residual(h, g, w_up_all, w_down_all, layer, g_final=None, cast_next=(), *, tm=512, tf=1024):
    S, D = h.shape
    Fd = w_up_all.shape[2]
    tm, tf = min(tm, S), min(tf, Fd)
    assert S % tm == 0 and Fd % tf == 0
    nf = Fd // tf
    n_steps = (S // tm) * nf
    final_norm = g_final is not None
    gf = (g_final if final_norm else g).reshape(1, D)
    cast_in, cast_out, cast_shapes = _cast_specs(cast_next, n_steps, lambda i, f: i * nf + f)
    outs = pl.pallas_call(
        functools.partial(_mlp_kernel, final_norm=final_norm, n_cast=len(cast_next)),
        grid=(S // tm, nf),
        in_specs=[pl.BlockSpec((tm, D), lambda i, f: (i, 0)),
                  pl.BlockSpec((1, D), lambda i, f: (0, 0)),
                  pl.BlockSpec((None, D, tf), lambda i, f: (layer, 0, f)),
                  pl.BlockSpec((None, tf, D), lambda i, f: (layer, f, 0)),
                  pl.BlockSpec((1, D), lambda i, f: (0, 0))] + cast_in,
        out_specs=[pl.BlockSpec((tm, D), lambda i, f: (i, 0))] + cast_out,
        out_shape=[jax.ShapeDtypeStruct((S, D), F32)] + cast_shapes,
        scratch_shapes=[pltpu.VMEM((tm, D), BF16)],
        compiler_params=_params("arbitrary", "arbitrary"),
        name="mlp_residual",
    )(h, g.reshape(1, D), w_up_all, w_down_all, gf, *[w for w, _ in cast_next])
    return outs[0], list(outs[1:])


def _moba_gate_kernel(*refs, nb, tq, n_cast):
    q_ref, k_ref = refs[:2]
    src_refs = refs[2:2 + n_cast]
    qaug_ref = refs[2 + n_cast]
    dst_refs = refs[3 + n_cast:3 + 2 * n_cast]
    kmean_ref = refs[3 + 2 * n_cast]
    blk = MOBA_BLOCK
    i = pl.program_id(1)

    for src_ref, dst_ref in zip(src_refs, dst_refs):
        dst_ref[...] = src_ref[...].astype(dst_ref.dtype)

    @pl.when(i == 0)
    def _():
        for n in range(nb):
            kn = k_ref[n * blk:(n + 1) * blk, :].astype(F32)
            kmean_ref[n:n + 1, :] = jnp.mean(kn, axis=0, keepdims=True)

    q = q_ref[...]
    km = kmean_ref[...]
    km_hi = km.astype(BF16)
    r1 = km - km_hi.astype(F32)
    km_mid = r1.astype(BF16)
    km_lo = (r1 - km_mid.astype(F32)).astype(BF16)
    gate = _dot_nt(km_hi, q) + _dot_nt(km_mid, q) + _dot_nt(km_lo, q)

    blk_id = lax.broadcasted_iota(jnp.int32, (nb, tq), 0)
    tok = lax.broadcasted_iota(jnp.int32, (nb, tq), 1)
    qb = i * (tq // blk) + lax.shift_right_logical(tok, jnp.int32(blk.bit_length() - 1))
    gate = jnp.where(blk_id < qb, gate, NEG_INF)
    chosen = jnp.zeros((nb, tq), jnp.bool_)
    for n in range(min(MOBA_TOP_K, nb)):
        best = jnp.max(gate, axis=0, keepdims=True)
        idx = jnp.min(jnp.where(gate == best, blk_id, nb), axis=0, keepdims=True)
        hit = blk_id == idx
        chosen = chosen | (hit & (n < qb))
        gate = jnp.where(hit, -jnp.inf, gate)

    eye = (lax.broadcasted_iota(jnp.int32, (nb, LANES), 0)
           == lax.broadcasted_iota(jnp.int32, (nb, LANES), 1))
    picked = _dot_tn(jnp.where(chosen, 1.0, 0.0).astype(BF16), jnp.where(eye, 1.0, 0.0).astype(BF16))
    lane = lax.broadcasted_iota(jnp.int32, (tq, LANES), 1)
    qaug_ref[:, 0:HEAD_DIM] = q
    qaug_ref[:, HEAD_DIM:2 * HEAD_DIM] = jnp.where((picked > 0.5) | (lane >= nb), 0.0, NEG_INF).astype(BF16)


def _moba_slots(nb, chunk):
    return max(-(-p // chunk) + -(-(nb - 1 - p) // chunk) for p in range(nb // 2))


def _moba_attn_kernel(qlo_ref, qhi_ref, k_ref, v_ref, olo_ref, ohi_ref, ka_ref, va_ref, q_ref, s_ref, own_ref,
                      m_ref, acc_ref, *, chunk, nb, heads):
    blk = MOBA_BLOCK
    span = chunk * blk
    last_chunk = nb // chunk - 1
    n_slots = _moba_slots(nb, chunk)
    p = pl.program_id(1)
    hd2 = 2 * HEAD_DIM

    @pl.when(jnp.logical_and(pl.program_id(0) == 0, p == 0))
    def _():
        lane = lax.broadcasted_iota(jnp.int32, (blk, LANES), 1)
        ones_col = jnp.where(lane == 0, 1.0, 0.0).astype(BF16)
        for n in range(nb):
            rows = slice(n * blk, (n + 1) * blk)
            for g in range(heads):
                ka_ref[g, rows, HEAD_DIM:hd2] = jnp.where(lane == n, 1.0, 0.0).astype(BF16)
                va_ref[g, rows, HEAD_DIM:hd2] = ones_col

    @pl.when(p == 0)
    def _():
        for g in range(heads):
            ka_ref[g, :, 0:HEAD_DIM] = k_ref[:, g * HEAD_DIM:(g + 1) * HEAD_DIM]
            va_ref[g, :, 0:HEAD_DIM] = v_ref[:, g * HEAD_DIM:(g + 1) * HEAD_DIM]

    q_ref[0] = qlo_ref[...]
    q_ref[1] = qhi_ref[...]

    qbs = (p, nb - 1 - p)
    q_refs, o_refs = (qlo_ref, qhi_ref), (olo_ref, ohi_ref)
    log2_scale = HEAD_DIM ** -0.5 * math.log2(math.e)
    n_lo = (qbs[0] + chunk - 1) // chunk
    n_hi = (qbs[1] + chunk - 1) // chunk

    def slot(c):
        is_hi = c < n_hi
        is_lo = jnp.logical_and(c >= n_hi, c < n_hi + n_lo)
        ck = jnp.where(is_hi, c, jnp.where(is_lo, c - n_hi, last_chunk))
        return is_hi, jnp.where(is_hi, 1, 0), pl.multiple_of(ck * span, span)

    row = lax.broadcasted_iota(jnp.int32, (blk, blk), 0)
    col = lax.broadcasted_iota(jnp.int32, (blk, blk), 1)
    starts = [pl.multiple_of(qb * blk, blk) for qb in qbs]
    for t in range(2):
        for g in range(heads):
            q = q_refs[t][:, g * hd2:g * hd2 + HEAD_DIM]
            s = _dot_nt(q, ka_ref[g, pl.ds(starts[t], blk), 0:HEAD_DIM]) * log2_scale
            s = jnp.where(col <= row, s, NEG_INF)
            own_ref[g, t] = s
            m_ref[g, t] = functools.reduce(jnp.maximum, _lane_groups(s))
    for c in range(n_slots):
        is_hi, which, st = slot(c)
        for g in range(heads):
            cols = slice(g * hd2, (g + 1) * hd2)
            s = _dot_nt(q_ref[which, :, cols], ka_ref[g, pl.ds(st, span), :]) * log2_scale
            s_ref[g, c] = s
            m_ref[g, which] = functools.reduce(jnp.maximum, _lane_groups(s), m_ref[g, which])
    for t in range(2):
        for g in range(heads):
            m_ref[g, t] = jnp.broadcast_to(jnp.max(m_ref[g, t], axis=1, keepdims=True), (blk, LANES))

    for t in range(2):
        for g in range(heads):
            m = m_ref[g, t]
            pr = [jnp.exp2(s - m) for s in _lane_groups(own_ref[g, t])]
            acc_ref[g, t] = _dot(jnp.concatenate(pr, axis=1).astype(BF16), va_ref[g, pl.ds(starts[t], blk), :])
    for c in range(n_slots):
        _, which, st = slot(c)
        for g in range(heads):
            m = m_ref[g, which]
            pr = [jnp.exp2(s - m) for s in _lane_groups(s_ref[g, c])]
            acc_ref[g, which] += _dot(jnp.concatenate(pr, axis=1).astype(BF16), va_ref[g, pl.ds(st, span), :])
    for t in range(2):
        for g in range(heads):
            acc = acc_ref[g, t]
            out = acc[:, 0:HEAD_DIM] * (1.0 / acc[:, HEAD_DIM:HEAD_DIM + 1])
            o_refs[t][:, g * HEAD_DIM:(g + 1) * HEAD_DIM] = out.astype(o_refs[t].dtype)


def moba_attention(qk, v, n_heads, cast=(), *, tq=4096, heads=2):
    S = qk.shape[0]
    blk, hd, H, chunk = MOBA_BLOCK, HEAD_DIM, n_heads, MOBA_CHUNK
    tq = min(tq, S)
    assert S % tq == 0 and tq % blk == 0 and hd == LANES and blk & (blk - 1) == 0
    nb = S // blk
    assert nb <= LANES and nb % 8 == 0 and nb % (2 * chunk) == 0
    half = nb // 2
    nq = S // tq
    cast_in, cast_out, cast_shapes = _cast_specs(cast, H * nq, lambda h, i: h * nq + i)
    qaug, *casts = pl.pallas_call(
        functools.partial(_moba_gate_kernel, nb=nb, tq=tq, n_cast=len(cast)),
        grid=(H, nq),
        in_specs=[pl.BlockSpec((tq, hd), lambda h, i: (i, h)),
                  pl.BlockSpec((S, hd), lambda h, i: (0, H + h))] + cast_in,
        out_specs=[pl.BlockSpec((tq, 2 * hd), lambda h, i: (i, h))] + cast_out,
        out_shape=[jax.ShapeDtypeStruct((S, H * 2 * hd), BF16)] + cast_shapes,
        scratch_shapes=[pltpu.VMEM((nb, hd), F32)],
        compiler_params=_params("arbitrary", "arbitrary"),
        name="moba_gate",
    )(qk, qk, *[w for w, _ in cast])
    out = jax.ShapeDtypeStruct((S // 2, H * hd), BF16)
    G = heads
    assert H % G == 0
    halves = pl.pallas_call(
        functools.partial(_moba_attn_kernel, chunk=chunk, nb=nb, heads=G),
        grid=(H // G, half),
        in_specs=[pl.BlockSpec((blk, G * 2 * hd), lambda h, p: (p, h)),
                  pl.BlockSpec((blk, G * 2 * hd), lambda h, p: (nb - 1 - p, h)),
                  pl.BlockSpec((S, G * hd), lambda h, p: (0, H // G + h)),
                  pl.BlockSpec((S, G * hd), lambda h, p: (0, h))],
        out_specs=[pl.BlockSpec((blk, G * hd), lambda h, p: (p, h)),
                   pl.BlockSpec((blk, G * hd), lambda h, p: (half - 1 - p, h))],
        out_shape=[out, out],
        scratch_shapes=[pltpu.VMEM((G, S, 2 * hd), BF16),
                        pltpu.VMEM((G, S, 2 * hd), BF16),
                        pltpu.VMEM((2, blk, G * 2 * hd), BF16),
                        pltpu.VMEM((G, _moba_slots(nb, chunk), blk, chunk * blk), F32),
                        pltpu.VMEM((G, 2, blk, blk), F32),
                        pltpu.VMEM((G, 2, blk, LANES), F32),
                        pltpu.VMEM((G, 2, blk, 2 * hd), F32)],
        compiler_params=pltpu.CompilerParams(dimension_semantics=("arbitrary", "arbitrary"),
                                             vmem_limit_bytes=MOBA_VMEM_LIMIT),
        name="moba_attention",
    )(qaug, qaug, qk, v)
    return halves, casts


TAIL = 8


def _conv_mix_kernel(x_ref, g_ref, wb_ref, wc_ref, wu_ref, cw_ref, a_ref, xn_ref, tail_ref):
    i, j = pl.program_id(0), pl.program_id(1)

    @pl.when(j == 0)
    def _():
        xn_ref[...] = _rmsnorm_rows(x_ref[...], g_ref[...]).astype(BF16)

    xn = xn_ref[...]
    z = _dot(xn, wc_ref[...]) * _dot(xn, wu_ref[...])
    tm = z.shape[0]
    prev = jnp.where(i == 0, 0.0, tail_ref[j])
    row = lax.broadcasted_iota(jnp.int32, z.shape, 0)
    zm1 = jnp.where(row == 0, prev[TAIL - 1:TAIL, :], pltpu.roll(z, 1, 0))
    zm2 = jnp.where(row == 0, prev[TAIL - 2:TAIL - 1, :],
                    jnp.where(row == 1, prev[TAIL - 1:TAIL, :], pltpu.roll(z, 2, 0)))
    cw = cw_ref[...]
    conv = cw[0:1, :] * zm2 + cw[1:2, :] * zm1 + cw[2:3, :] * z
    a_ref[...] = (_dot(xn, wb_ref[...]) * conv).astype(a_ref.dtype)
    tail_ref[j] = z[tm - TAIL:tm, :]


def conv_mix(x, g, w_in_all, layer, conv_w, *, tm=512, tn=1024):
    S, D = x.shape
    tm, tn = min(tm, S), min(tn, D)
    assert S % tm == 0 and D % tn == 0 and w_in_all.shape[2] == 3 * D and tm % TAIL == 0
    assert conv_w.shape == (CONV_WIDTH, D)
    nj = D // tn
    return pl.pallas_call(
        _conv_mix_kernel,
        grid=(S // tm, nj),
        in_specs=[pl.BlockSpec((tm, D), lambda i, j: (i, 0)),
                  pl.BlockSpec((1, D), lambda i, j: (0, 0)),
                  pl.BlockSpec((None, D, tn), lambda i, j: (layer, 0, j)),
                  pl.BlockSpec((None, D, tn), lambda i, j: (layer, 0, nj + j)),
                  pl.BlockSpec((None, D, tn), lambda i, j: (layer, 0, 2 * nj + j)),
                  pl.BlockSpec((CONV_WIDTH, tn), lambda i, j: (0, j))],
        out_specs=pl.BlockSpec((tm, tn), lambda i, j: (i, j)),
        out_shape=jax.ShapeDtypeStruct((S, D), BF16),
        scratch_shapes=[pltpu.VMEM((tm, D), BF16), pltpu.VMEM((nj, TAIL, tn), F32)],
        compiler_params=_params("arbitrary", "arbitrary"),
        name="conv_mix",
    )(x, g.reshape(1, D), w_in_all, w_in_all, w_in_all, conv_w)


def _dilated_kernel(q_ref, k_ref, v_ref, o_ref, lse_ref, kband_ref, vband_ref, *, band, n_heads, blocks):
    j = pl.program_id(1)
    scale = HEAD_DIM ** -0.5

    @pl.when(j == 0)
    def _():
        kband_ref[0:band, :] = jnp.zeros((band, kband_ref.shape[1]), BF16)
        vband_ref[0:band, :] = jnp.zeros((band, vband_ref.shape[1]), BF16)

    kband_ref[band:(blocks + 1) * band, :] = k_ref[...]
    vband_ref[band:(blocks + 1) * band, :] = v_ref[...]
    a = lax.broadcasted_iota(jnp.int32, (band, 2 * band), 0)
    b = lax.broadcasted_iota(jnp.int32, (band, 2 * band), 1)
    in_band = (b >= a) & (b <= a + band)
    for u in range(blocks):
        rows = slice(u * band, (u + 1) * band)
        keys = slice(u * band, (u + 2) * band)
        valid = in_band & ((j > 0) | (b >= band)) if u == 0 else in_band
        for hh in range(n_heads):
            sl = slice(hh * HEAD_DIM, (hh + 1) * HEAD_DIM)
            s = _dot_nt(q_ref[rows, sl], kband_ref[keys, sl]) * scale
            s = jnp.where(valid, s, NEG_INF)
            m = jnp.max(s, axis=1, keepdims=True)
            e = jnp.exp(s - m)
            l = jnp.sum(e, axis=1, keepdims=True)
            o_ref[rows, sl] = (_dot(e.astype(BF16), vband_ref[keys, sl]) * (1.0 / l)).astype(o_ref.dtype)
            lse_ref[rows, sl] = jnp.broadcast_to(m + jnp.log(l), (band, HEAD_DIM))
    kband_ref[0:band, :] = k_ref[(blocks - 1) * band:blocks * band, :]
    vband_ref[0:band, :] = v_ref[(blocks - 1) * band:blocks * band, :]


def dilated_group_attention(qk, v, window, dilation, n_heads, *, blocks=2):
    dil, L, W = qk.shape
    band = window // dilation
    width = n_heads * HEAD_DIM
    assert dil == dilation and W == 2 * width and v.shape == (dil, L, width)
    blocks = min(blocks, L // band)
    rows = blocks * band
    assert L % rows == 0

    def spec(t):
        return pl.BlockSpec((None, rows, width), lambda r, j: (r, j, t))

    return pl.pallas_call(
        functools.partial(_dilated_kernel, band=band, n_heads=n_heads, blocks=blocks),
        grid=(dil, L // rows),
        in_specs=[spec(0), spec(1), spec(0)],
        out_specs=[pl.BlockSpec((None, rows, width), lambda r, j: (r, j, 0))] * 2,
        out_shape=[jax.ShapeDtypeStruct((dil, L, width), BF16),
                   jax.ShapeDtypeStruct((dil, L, width), F32)],
        scratch_shapes=[pltpu.VMEM(((blocks + 1) * band, width), BF16)] * 2,
        compiler_params=_params("parallel", "arbitrary"),
        name="dilated_attention",
    )(qk, qk, v)


def _combine_kernel(*refs, dilations):
    n = len(dilations)
    o_refs, l_refs, out_ref = refs[:n], refs[n:2 * n], refs[2 * n]
    scratch = list(refs[2 * n + 1:])
    tm, width = out_ref.shape

    views = []
    for g, dil in enumerate(dilations):
        if dil == 1:
            views.append((lambda sl, o=o_refs[g]: o[0, :, sl].astype(F32), lambda sl, l=l_refs[g]: l[0, :, sl]))
            continue
        o_scr, l_scr = scratch.pop(0), scratch.pop(0)
        rows = tm // dil
        for c in range(width // LANES):
            sl = slice(c * LANES, (c + 1) * LANES)
            for r in range(dil):
                o_scr[c, pl.ds(r, rows, stride=dil), :] = o_refs[g][r, :, sl].astype(F32)
                l_scr[c, pl.ds(r, rows, stride=dil), :] = l_refs[g][r, :, sl]
        views.append((lambda sl, s=o_scr: s[sl.start // LANES], lambda sl, s=l_scr: s[sl.start // LANES]))

    for c in range(width // LANES):
        sl = slice(c * LANES, (c + 1) * LANES)
        lses = [lv(sl) for _, lv in views]
        top = functools.reduce(jnp.maximum, lses)
        ws = [jnp.exp(l - top) for l in lses]
        inv = 1.0 / functools.reduce(jnp.add, ws)
        acc = (ws[0] * inv) * views[0][0](sl)
        for w, (ov, _) in zip(ws[1:], views[1:]):
            acc = acc + (w * inv) * ov(sl)
        out_ref[:, sl] = acc.astype(out_ref.dtype)


def combine_groups(outs, lses, dilations, *, tm=256):
    width = outs[0].shape[2]
    S = outs[0].shape[0] * outs[0].shape[1]
    tm = min(tm, S)
    assert S % tm == 0 and all(tm % (8 * d) == 0 for d in dilations)
    specs = [pl.BlockSpec((d, tm // d, width), lambda i: (0, i, 0)) for d in dilations]
    scratch = []
    for d in dilations:
        if d > 1:
            scratch += [pltpu.VMEM((width // LANES, tm, LANES), F32)] * 2
    return pl.pallas_call(
        functools.partial(_combine_kernel, dilations=tuple(dilations)),
        grid=(S // tm,),
        in_specs=specs + specs,
        out_specs=pl.BlockSpec((tm, width), lambda i: (i, 0)),
        out_shape=jax.ShapeDtypeStruct((S, width), BF16),
        scratch_shapes=scratch,
        compiler_params=_params("parallel"),
        name="combine_groups",
    )(*outs, *lses)


def kernel(x, positions, norm_mix, norm_mlp, norm_final, mlp_w_up, mlp_w_down, moba_w_qkv, moba_w_o,
           conv_w_in, conv_w, conv_w_out, dil_w_qkv, dil_w_o):
    B, S, D = x.shape
    assert B == 1 and D % HEAD_DIM == 0
    n_heads = D // HEAD_DIM
    depth = norm_mix.shape[0]
    h = x[0]
    tables = rope_tables(positions[0])
    mixers = ((moba_w_qkv, moba_w_o), (conv_w_in, conv_w_out), (dil_w_qkv, dil_w_o))

    def f32_weights(i):
        kind, j = i % N_MIXERS, i // N_MIXERS
        return [(mixers[kind][0], j), (mixers[kind][1], j), (mlp_w_up, i), (mlp_w_down, i)]

    (w0, idx0), *late0 = f32_weights(0)
    weights = [w0[idx0:idx0 + 1].astype(BF16), None, None, None]
    for i in range(depth):
        kind = i % N_MIXERS
        w_in, w_out, w_up, w_down = weights
        if kind == 0:
            qk, v = norm_matmul_rope(h, norm_mix[i], w_in, 0, tables)
            halves, casts = moba_attention(qk[0], v[0], n_heads, late0 if i == 0 else ())
            if i == 0:
                w_out, w_up, w_down = casts
            h = matmul_residual(halves, w_out, 0, h)
        elif kind == 1:
            a = conv_mix(h, norm_mix[i], w_in, 0, conv_w[i // N_MIXERS])
            h = matmul_residual([a], w_out, 0, h)
        else:
            outs, lses = [], []
            for g, (window, dilation) in enumerate(DIL_GROUPS):
                qk, v = norm_matmul_rope(h, norm_mix[i], w_in, 0, tables, sec0=3 * g, dilation=dilation)
                o_g, lse_g = dilated_group_attention(qk, v, window, dilation, n_heads)
                outs.append(o_g)
                lses.append(lse_g)
            o = combine_groups(outs, lses, [d for _, d in DIL_GROUPS])
            h = matmul_residual([o], w_out, 0, h)
        last = i == depth - 1
        h, weights = mlp_residual(h, norm_mlp[i], w_up, w_down, 0, norm_final if last else None,
                                  () if last else f32_weights(i + 1))
    return h[None]
```

```python
import functools
import math

import jax
import jax.numpy as jnp
from jax import lax
from jax.experimental import pallas as pl
from jax.experimental.pallas import tpu as pltpu

HEAD_DIM = 128
MOBA_BLOCK = 256
MOBA_TOP_K = 3
MOBA_CHUNK = 2
CONV_WIDTH = 3
DIL_GROUPS = ((128, 1), (512, 4), (2048, 16))
ROT_DIM = HEAD_DIM // 4
ROPE_THETA = 500000.0
RMS_EPS = 1e-5
NEG_INF = -1e30
N_MIXERS = 3

LANES = 128
VMEM_LIMIT = 56 * 1024 * 1024
MOBA_VMEM_LIMIT = 60 * 1024 * 1024

F32 = jnp.float32
BF16 = jnp.bfloat16


def _params(*sem):
    return pltpu.CompilerParams(dimension_semantics=sem, vmem_limit_bytes=VMEM_LIMIT)


def _rmsnorm_rows(x, g):
    y = x * lax.rsqrt(jnp.mean(x * x, axis=-1, keepdims=True) + RMS_EPS)
    return y * g


def _dot(a, b):
    return jnp.dot(a, b, preferred_element_type=F32)


def _dot_nt(a, b):
    return lax.dot_general(a, b, (((1,), (1,)), ((), ())), preferred_element_type=F32)


def _dot_tn(a, b):
    return lax.dot_general(a, b, (((0,), (0,)), ((), ())), preferred_element_type=F32)


def _lane_groups(x):
    return [x[:, g * LANES:(g + 1) * LANES] for g in range(x.shape[1] // LANES)]


def _rope_table_kernel(pos_ref, invf_ref, cos_ref, sa_ref, sb_ref):
    ang = pos_ref[...] * invf_ref[...]
    lane = lax.broadcasted_iota(jnp.int32, ang.shape, 1)
    c, s = jnp.cos(ang), jnp.sin(ang)
    half = ROT_DIM // 2
    cos_ref[...] = jnp.where(lane < ROT_DIM, c, 1.0)
    sa_ref[...] = jnp.where(lane < half, -s, 0.0)
    sb_ref[...] = jnp.where((lane >= half) & (lane < ROT_DIM), s, 0.0)


def rope_tables(positions):
    S = positions.shape[0]
    tm = min(S, 1024)
    half = ROT_DIM // 2
    inv_freq = ROPE_THETA ** (-jnp.arange(0, ROT_DIM, 2, dtype=F32) / ROT_DIM)
    invf = jnp.concatenate([inv_freq, inv_freq, jnp.zeros((LANES - 2 * half,), F32)])[None, :]
    pos = positions.astype(F32)[:, None]
    tab = jax.ShapeDtypeStruct((S, LANES), F32)
    return pl.pallas_call(
        _rope_table_kernel,
        grid=(S // tm,),
        in_specs=[pl.BlockSpec((tm, 1), lambda i: (i, 0)),
                  pl.BlockSpec((1, LANES), lambda i: (0, 0))],
        out_specs=[pl.BlockSpec((tm, LANES), lambda i: (i, 0))] * 3,
        out_shape=[tab, tab, tab],
        compiler_params=_params("arbitrary"),
        name="rope_tables",
    )(pos, invf)


def _apply_rope(t, cos, sa, sb):
    half = ROT_DIM // 2
    return (t * cos + pltpu.roll(t, LANES - half, 1) * sa + pltpu.roll(t, half, 1) * sb)


def _norm_mm_rope_kernel(x_ref, g_ref, wr_ref, wv_ref, cos_ref, sa_ref, sb_ref, oqk_ref, ov_ref, xn_ref,
                         *scratch, dilation):
    @pl.when(pl.program_id(1) == 0)
    def _():
        xn_ref[...] = _rmsnorm_rows(x_ref[...], g_ref[...]).astype(BF16)

    xn = xn_ref[...]
    acc_r = _dot(xn, wr_ref[...])
    acc_v = _dot(xn, wv_ref[...])
    heads_r, heads_v = acc_r.shape[1] // HEAD_DIM, acc_v.shape[1] // HEAD_DIM
    if dilation == 1:
        cos, sa, sb = cos_ref[...], sa_ref[...], sb_ref[...]
        for c in range(heads_r):
            sl = slice(c * HEAD_DIM, (c + 1) * HEAD_DIM)
            oqk_ref[0, :, sl] = _apply_rope(acc_r[:, sl], cos, sa, sb).astype(oqk_ref.dtype)
        ov_ref[0] = acc_v.astype(ov_ref.dtype)
    else:
        acc_ref, = scratch
        for c in range(heads_r):
            acc_ref[c] = acc_r[:, c * HEAD_DIM:(c + 1) * HEAD_DIM]
        for c in range(heads_v):
            acc_ref[heads_r + c] = acc_v[:, c * HEAD_DIM:(c + 1) * HEAD_DIM]
        rows = acc_r.shape[0] // dilation
        for r in range(dilation):
            pick = pl.ds(r, rows, stride=dilation)
            cos, sa, sb = cos_ref[pick, :], sa_ref[pick, :], sb_ref[pick, :]
            for c in range(heads_r):
                sl = slice(c * HEAD_DIM, (c + 1) * HEAD_DIM)
                oqk_ref[r, :, sl] = _apply_rope(acc_ref[c, pick, :], cos, sa, sb).astype(oqk_ref.dtype)
        for r in range(dilation):
            pick = pl.ds(r, rows, stride=dilation)
            for c in range(heads_v):
                sl = slice(c * HEAD_DIM, (c + 1) * HEAD_DIM)
                ov_ref[r, :, sl] = acc_ref[heads_r + c, pick, :].astype(ov_ref.dtype)


def norm_matmul_rope(x, g, w_all, layer, tables, *, sec0=0, dilation=1, tm=512):
    S, D = x.shape
    tm = min(tm, S)
    assert S % tm == 0 and tm % (16 * dilation) == 0 and D % (2 * HEAD_DIM) == 0
    rows = tm // dilation
    half = D // 2
    scratch = [pltpu.VMEM((tm, D), BF16)]
    if dilation > 1:
        scratch.append(pltpu.VMEM(((D + half) // HEAD_DIM, tm, HEAD_DIM), F32))
    L = S // dilation
    return pl.pallas_call(
        functools.partial(_norm_mm_rope_kernel, dilation=dilation),
        grid=(S // tm, 2),
        in_specs=[pl.BlockSpec((tm, D), lambda i, j: (i, 0)),
                  pl.BlockSpec((1, D), lambda i, j: (0, 0)),
                  pl.BlockSpec((None, D, D), lambda i, j: (layer, 0, sec0 + j)),
                  pl.BlockSpec((None, D, half), lambda i, j: (layer, 0, 2 * (sec0 + 2) + j))]
                 + [pl.BlockSpec((tm, LANES), lambda i, j: (i, 0))] * 3,
        out_specs=[pl.BlockSpec((dilation, rows, D), lambda i, j: (0, i, j)),
                   pl.BlockSpec((dilation, rows, half), lambda i, j: (0, i, j))],
        out_shape=[jax.ShapeDtypeStruct((dilation, L, 2 * D), BF16),
                   jax.ShapeDtypeStruct((dilation, L, D), BF16)],
        scratch_shapes=scratch,
        compiler_params=_params("parallel", "arbitrary"),
        name="norm_matmul_rope",
    )(x, g.reshape(1, D), w_all, w_all, *tables)


def _mm_res_kernel(*refs, n_parts, tiles_per_part):
    a_refs, (w_ref, h_ref, o_ref) = refs[:n_parts], refs[n_parts:]
    part = pl.program_id(0) // tiles_per_part
    for p, a_ref in enumerate(a_refs):
        @pl.when(part == p)
        def _(a_ref=a_ref):
            o_ref[...] = h_ref[...] + _dot(a_ref[...], w_ref[...])


def matmul_residual(a_parts, w_all, layer, h, *, tm=512, tn=2048):
    n_parts = len(a_parts)
    Sp, K = a_parts[0].shape
    S, N = h.shape
    tm, tn = min(tm, Sp), min(tn, N)
    assert Sp * n_parts == S and Sp % tm == 0 and N % tn == 0
    tiles = Sp // tm

    def a_spec(p):
        return pl.BlockSpec((tm, K), lambda i, j: (jnp.clip(i - p * tiles, 0, tiles - 1), 0))

    return pl.pallas_call(
        functools.partial(_mm_res_kernel, n_parts=n_parts, tiles_per_part=tiles),
        grid=(S // tm, N // tn),
        in_specs=[a_spec(p) for p in range(n_parts)]
                 + [pl.BlockSpec((None, K, tn), lambda i, j: (layer, 0, j)),
                    pl.BlockSpec((tm, tn), lambda i, j: (i, j))],
        out_specs=pl.BlockSpec((tm, tn), lambda i, j: (i, j)),
        out_shape=jax.ShapeDtypeStruct((S, N), F32),
        compiler_params=_params("parallel", "arbitrary"),
        name="matmul_residual",
    )(*a_parts, w_all, h)


def _cast_specs(jobs, n_steps, step_of):
    ins, outs, shapes = [], [], []
    for w_all, idx in jobs:
        _, K, N = w_all.shape
        rows = K // n_steps
        assert rows * n_steps == K and rows % 16 == 0
        ins.append(pl.BlockSpec((None, rows, N), lambda i, j, idx=idx: (idx, step_of(i, j), 0)))
        outs.append(pl.BlockSpec((None, rows, N), lambda i, j: (0, step_of(i, j), 0)))
        shapes.append(jax.ShapeDtypeStruct((1, K, N), BF16))
    return ins, outs, shapes


def _mlp_kernel(*refs, final_norm, n_cast):
    h_ref, g_ref, wu_ref, wd_ref, gf_ref = refs[:5]
    src_refs = refs[5:5 + n_cast]
    o_ref = refs[5 + n_cast]
    dst_refs = refs[6 + n_cast:6 + 2 * n_cast]
    xn_ref = refs[6 + 2 * n_cast]
    f = pl.program_id(1)

    @pl.when(f == 0)
    def _():
        h = h_ref[...]
        xn_ref[...] = _rmsnorm_rows(h, g_ref[...]).astype(BF16)
        o_ref[...] = h

    hid = jnp.maximum(_dot(xn_ref[...], wu_ref[...]), 0.0)
    o_ref[...] += _dot((hid * hid).astype(BF16), wd_ref[...])

    if final_norm:
        @pl.when(f == pl.num_programs(1) - 1)
        def _():
            o_ref[...] = _rmsnorm_rows(o_ref[...], gf_ref[...])

    for src_ref, dst_ref in zip(src_refs, dst_refs):
        dst_ref[...] = src_ref[...].astype(dst_ref.dtype)


def mlp_residual(h, g, w_up_all, w_down_all, layer, g_final=None, cast_next=(), *, tm=512, tf=1024):
    S, D = h.shape
    Fd = w_up_all.shape[2]
    tm, tf = min(tm, S), min(tf, Fd)
    assert S % tm == 0 and Fd % tf == 0
    nf = Fd // tf
    n_steps = (S // tm) * nf
    final_norm = g_final is not None
    gf = (g_final if final_norm else g).reshape(1, D)
    cast_in, cast_out, cast_shapes = _cast_specs(cast_next, n_steps, lambda i, f: i * nf + f)
    outs = pl.pallas_call(
        functools.partial(_mlp_kernel, final_norm=final_norm, n_cast=len(cast_next)),
        grid=(S // tm, nf),
        in_specs=[pl.BlockSpec((tm, D), lambda i, f: (i, 0)),
                  pl.BlockSpec((1, D), lambda i, f: (0, 0)),
                  pl.BlockSpec((None, D, tf), lambda i, f: (layer, 0, f)),
                  pl.BlockSpec((None, tf, D), lambda i, f: (layer, f, 0)),
                  pl.BlockSpec((1, D), lambda i, f: (0, 0))] + cast_in,
        out_specs=[pl.BlockSpec((tm, D), lambda i, f: (i, 0))] + cast_out,
        out_shape=[jax.ShapeDtypeStruct((S, D), F32)] + cast_shapes,
        scratch_shapes=[pltpu.VMEM((tm, D), BF16)],
        compiler_params=_params("arbitrary", "arbitrary"),
        name="mlp_residual",
    )(h, g.reshape(1, D), w_up_all, w_down_all, gf, *[w for w, _ in cast_next])
    return outs[0], list(outs[1:])


def _moba_gate_kernel(*refs, nb, tq, n_cast):
    q_ref, k_ref = refs[:2]
    src_refs = refs[2:2 + n_cast]
    qaug_ref = refs[2 + n_cast]
    dst_refs = refs[3 + n_cast:3 + 2 * n_cast]
    kmean_ref = refs[3 + 2 * n_cast]
    blk = MOBA_BLOCK
    i = pl.program_id(1)

    for src_ref, dst_ref in zip(src_refs, dst_refs):
        dst_ref[...] = src_ref[...].astype(dst_ref.dtype)

    @pl.when(i == 0)
    def _():
        for n in range(nb):
            kn = k_ref[n * blk:(n + 1) * blk, :].astype(F32)
            kmean_ref[n:n + 1, :] = jnp.mean(kn, axis=0, keepdims=True)

    q = q_ref[...]
    km = kmean_ref[...]
    km_hi = km.astype(BF16)
    r1 = km - km_hi.astype(F32)
    km_mid = r1.astype(BF16)
    km_lo = (r1 - km_mid.astype(F32)).astype(BF16)
    gate = _dot_nt(km_hi, q) + _dot_nt(km_mid, q) + _dot_nt(km_lo, q)

    blk_id = lax.broadcasted_iota(jnp.int32, (nb, tq), 0)
    tok = lax.broadcasted_iota(jnp.int32, (nb, tq), 1)
    qb = i * (tq // blk) + lax.shift_right_logical(tok, jnp.int32(blk.bit_length() - 1))
    gate = jnp.where(blk_id < qb, gate, NEG_INF)
    chosen = jnp.zeros((nb, tq), jnp.bool_)
    for n in range(min(MOBA_TOP_K, nb)):
        best = jnp.max(gate, axis=0, keepdims=True)
        idx = jnp.min(jnp.where(gate == best, blk_id, nb), axis=0, keepdims=True)
        hit = blk_id == idx
        chosen = chosen | (hit & (n < qb))
        gate = jnp.where(hit, -jnp.inf, gate)

    eye = (lax.broadcasted_iota(jnp.int32, (nb, LANES), 0)
           == lax.broadcasted_iota(jnp.int32, (nb, LANES), 1))
    picked = _dot_tn(jnp.where(chosen, 1.0, 0.0).astype(BF16), jnp.where(eye, 1.0, 0.0).astype(BF16))
    lane = lax.broadcasted_iota(jnp.int32, (tq, LANES), 1)
    qaug_ref[:, 0:HEAD_DIM] = q
    qaug_ref[:, HEAD_DIM:2 * HEAD_DIM] = jnp.where((picked > 0.5) | (lane >= nb), 0.0, NEG_INF).astype(BF16)


def _moba_slots(nb, chunk):
    return max(-(-p // chunk) + -(-(nb - 1 - p) // chunk) for p in range(nb // 2))


def _moba_attn_kernel(qlo_ref, qhi_ref, k_ref, v_ref, olo_ref, ohi_ref, ka_ref, va_ref, q_ref, s_ref, own_ref,
                      m_ref, acc_ref, *, chunk, nb, heads):
    blk = MOBA_BLOCK
    span = chunk * blk
    last_chunk = nb // chunk - 1
    n_slots = _moba_slots(nb, chunk)
    p = pl.program_id(1)
    hd2 = 2 * HEAD_DIM

    @pl.when(jnp.logical_and(pl.program_id(0) == 0, p == 0))
    def _():
        lane = lax.broadcasted_iota(jnp.int32, (blk, LANES), 1)
        ones_col = jnp.where(lane == 0, 1.0, 0.0).astype(BF16)
        for n in range(nb):
            rows = slice(n * blk, (n + 1) * blk)
            for g in range(heads):
                ka_ref[g, rows, HEAD_DIM:hd2] = jnp.where(lane == n, 1.0, 0.0).astype(BF16)
                va_ref[g, rows, HEAD_DIM:hd2] = ones_col

    @pl.when(p == 0)
    def _():
        for g in range(heads):
            ka_ref[g, :, 0:HEAD_DIM] = k_ref[:, g * HEAD_DIM:(g + 1) * HEAD_DIM]
            va_ref[g, :, 0:HEAD_DIM] = v_ref[:, g * HEAD_DIM:(g + 1) * HEAD_DIM]

    q_ref[0] = qlo_ref[...]
    q_ref[1] = qhi_ref[...]

    qbs = (p, nb - 1 - p)
    q_refs, o_refs = (qlo_ref, qhi_ref), (olo_ref, ohi_ref)
    log2_scale = HEAD_DIM ** -0.5 * math.log2(math.e)
    n_lo = (qbs[0] + chunk - 1) // chunk
    n_hi = (qbs[1] + chunk - 1) // chunk

    def slot(c):
        is_hi = c < n_hi
        is_lo = jnp.logical_and(c >= n_hi, c < n_hi + n_lo)
        ck = jnp.where(is_hi, c, jnp.where(is_lo, c - n_hi, last_chunk))
        return is_hi, jnp.where(is_hi, 1, 0), pl.multiple_of(ck * span, span)

    row = lax.broadcasted_iota(jnp.int32, (blk, blk), 0)
    col = lax.broadcasted_iota(jnp.int32, (blk, blk), 1)
    starts = [pl.multiple_of(qb * blk, blk) for qb in qbs]
    for t in range(2):
        for g in range(heads):
            q = q_refs[t][:, g * hd2:g * hd2 + HEAD_DIM]
            s = _dot_nt(q, ka_ref[g, pl.ds(starts[t], blk), 0:HEAD_DIM]) * log2_scale
            s = jnp.where(col <= row, s, NEG_INF)
            own_ref[g, t] = s
            m_ref[g, t] = functools.reduce(jnp.maximum, _lane_groups(s))
    for c in range(n_slots):
        is_hi, which, st = slot(c)
        for g in range(heads):
            cols = slice(g * hd2, (g + 1) * hd2)
            s = _dot_nt(q_ref[which, :, cols], ka_ref[g, pl.ds(st, span), :]) * log2_scale
            s_ref[g, c] = s
            m_ref[g, which] = functools.reduce(jnp.maximum, _lane_groups(s), m_ref[g, which])
    for t in range(2):
        for g in range(heads):
            m_ref[g, t] = jnp.broadcast_to(jnp.max(m_ref[g, t], axis=1, keepdims=True), (blk, LANES))

    for t in range(2):
        for g in range(heads):
            m = m_ref[g, t]
            pr = [jnp.exp2(s - m) for s in _lane_groups(own_ref[g, t])]
            acc_ref[g, t] = _dot(jnp.concatenate(pr, axis=1).astype(BF16), va_ref[g, pl.ds(starts[t], blk), :])
    for c in range(n_slots):
        _, which, st = slot(c)
        for g in range(heads):
            m = m_ref[g, which]
            pr = [jnp.exp2(s - m) for s in _lane_groups(s_ref[g, c])]
            acc_ref[g, which] += _dot(jnp.concatenate(pr, axis=1).astype(BF16), va_ref[g, pl.ds(st, span), :])
    for t in range(2):
        for g in range(heads):
            acc = acc_ref[g, t]
            out = acc[:, 0:HEAD_DIM] * (1.0 / acc[:, HEAD_DIM:HEAD_DIM + 1])
            o_refs[t][:, g * HEAD_DIM:(g + 1) * HEAD_DIM] = out.astype(o_refs[t].dtype)


def moba_attention(qk, v, n_heads, cast=(), *, tq=4096, heads=2):
    S = qk.shape[0]
    blk, hd, H, chunk = MOBA_BLOCK, HEAD_DIM, n_heads, MOBA_CHUNK
    tq = min(tq, S)
    assert S % tq == 0 and tq % blk == 0 and hd == LANES and blk & (blk - 1) == 0
    nb = S // blk
    assert nb <= LANES and nb % 8 == 0 and nb % (2 * chunk) == 0
    half = nb // 2
    nq = S // tq
    cast_in, cast_out, cast_shapes = _cast_specs(cast, H * nq, lambda h, i: h * nq + i)
    qaug, *casts = pl.pallas_call(
        functools.partial(_moba_gate_kernel, nb=nb, tq=tq, n_cast=len(cast)),
        grid=(H, nq),
        in_specs=[pl.BlockSpec((tq, hd), lambda h, i: (i, h)),
                  pl.BlockSpec((S, hd), lambda h, i: (0, H + h))] + cast_in,
        out_specs=[pl.BlockSpec((tq, 2 * hd), lambda h, i: (i, h))] + cast_out,
        out_shape=[jax.ShapeDtypeStruct((S, H * 2 * hd), BF16)] + cast_shapes,
        scratch_shapes=[pltpu.VMEM((nb, hd), F32)],
        compiler_params=_params("arbitrary", "arbitrary"),
        name="moba_gate",
    )(qk, qk, *[w for w, _ in cast])
    out = jax.ShapeDtypeStruct((S // 2, H * hd), BF16)
    G = heads
    assert H % G == 0
    halves = pl.pallas_call(
        functools.partial(_moba_attn_kernel, chunk=chunk, nb=nb, heads=G),
        grid=(H // G, half),
        in_specs=[pl.BlockSpec((blk, G * 2 * hd), lambda h, p: (p, h)),
                  pl.BlockSpec((blk, G * 2 * hd), lambda h, p: (nb - 1 - p, h)),
                  pl.BlockSpec((S, G * hd), lambda h, p: (0, H // G + h)),
                  pl.BlockSpec((S, G * hd), lambda h, p: (0, h))],
        out_specs=[pl.BlockSpec((blk, G * hd), lambda h, p: (p, h)),
                   pl.BlockSpec((blk, G * hd), lambda h, p: (half - 1 - p, h))],
        out_shape=[out, out],
        scratch_shapes=[pltpu.VMEM((G, S, 2 * hd), BF16),
                        pltpu.VMEM((G, S, 2 * hd), BF16),
                        pltpu.VMEM((2, blk, G * 2 * hd), BF16),
                        pltpu.VMEM((G, _moba_slots(nb, chunk), blk, chunk * blk), F32),
                        pltpu.VMEM((G, 2, blk, blk), F32),
                        pltpu.VMEM((G, 2, blk, LANES), F32),
                        pltpu.VMEM((G, 2, blk, 2 * hd), F32)],
        compiler_params=pltpu.CompilerParams(dimension_semantics=("arbitrary", "arbitrary"),
                                             vmem_limit_bytes=MOBA_VMEM_LIMIT),
        name="moba_attention",
    )(qaug, qaug, qk, v)
    return halves, casts


TAIL = 8


def _conv_mix_kernel(x_ref, g_ref, wb_ref, wc_ref, wu_ref, cw_ref, a_ref, xn_ref, tail_ref):
    i, j = pl.program_id(0), pl.program_id(1)

    @pl.when(j == 0)
    def _():
        xn_ref[...] = _rmsnorm_rows(x_ref[...], g_ref[...]).astype(BF16)

    xn = xn_ref[...]
    z = _dot(xn, wc_ref[...]) * _dot(xn, wu_ref[...])
    tm = z.shape[0]
    prev = jnp.where(i == 0, 0.0, tail_ref[j])
    row = lax.broadcasted_iota(jnp.int32, z.shape, 0)
    zm1 = jnp.where(row == 0, prev[TAIL - 1:TAIL, :], pltpu.roll(z, 1, 0))
    zm2 = jnp.where(row == 0, prev[TAIL - 2:TAIL - 1, :],
                    jnp.where(row == 1, prev[TAIL - 1:TAIL, :], pltpu.roll(z, 2, 0)))
    cw = cw_ref[...]
    conv = cw[0:1, :] * zm2 + cw[1:2, :] * zm1 + cw[2:3, :] * z
    a_ref[...] = (_dot(xn, wb_ref[...]) * conv).astype(a_ref.dtype)
    tail_ref[j] = z[tm - TAIL:tm, :]


def conv_mix(x, g, w_in_all, layer, conv_w, *, tm=512, tn=1024):
    S, D = x.shape
    tm, tn = min(tm, S), min(tn, D)
    assert S % tm == 0 and D % tn == 0 and w_in_all.shape[2] == 3 * D and tm % TAIL == 0
    assert conv_w.shape == (CONV_WIDTH, D)
    nj = D // tn
    return pl.pallas_call(
        _conv_mix_kernel,
        grid=(S // tm, nj),
        in_specs=[pl.BlockSpec((tm, D), lambda i, j: (i, 0)),
                  pl.BlockSpec((1, D), lambda i, j: (0, 0)),
                  pl.BlockSpec((None, D, tn), lambda i, j: (layer, 0, j)),
                  pl.BlockSpec((None, D, tn), lambda i, j: (layer, 0, nj + j)),
                  pl.BlockSpec((None, D, tn), lambda i, j: (layer, 0, 2 * nj + j)),
                  pl.BlockSpec((CONV_WIDTH, tn), lambda i, j: (0, j))],
        out_specs=pl.BlockSpec((tm, tn), lambda i, j: (i, j)),
        out_shape=jax.ShapeDtypeStruct((S, D), BF16),
        scratch_shapes=[pltpu.VMEM((tm, D), BF16), pltpu.VMEM((nj, TAIL, tn), F32)],
        compiler_params=_params("arbitrary", "arbitrary"),
        name="conv_mix",
    )(x, g.reshape(1, D), w_in_all, w_in_all, w_in_all, conv_w)


def _dilated_kernel(q_ref, k_ref, v_ref, o_ref, lse_ref, kband_ref, vband_ref, *, band, n_heads, blocks):
    j = pl.program_id(1)
    scale = HEAD_DIM ** -0.5

    @pl.when(j == 0)
    def _():
        kband_ref[0:band, :] = jnp.zeros((band, kband_ref.shape[1]), BF16)
        vband_ref[0:band, :] = jnp.zeros((band, vband_ref.shape[1]), BF16)

    kband_ref[band:(blocks + 1) * band, :] = k_ref[...]
    vband_ref[band:(blocks + 1) * band, :] = v_ref[...]
    a = lax.broadcasted_iota(jnp.int32, (band, 2 * band), 0)
    b = lax.broadcasted_iota(jnp.int32, (band, 2 * band), 1)
    in_band = (b >= a) & (b <= a + band)
    for u in range(blocks):
        rows = slice(u * band, (u + 1) * band)
        keys = slice(u * band, (u + 2) * band)
        valid = in_band & ((j > 0) | (b >= band)) if u == 0 else in_band
        for hh in range(n_heads):
            sl = slice(hh * HEAD_DIM, (hh + 1) * HEAD_DIM)
            s = _dot_nt(q_ref[rows, sl], kband_ref[keys, sl]) * scale
            s = jnp.where(valid, s, NEG_INF)
            m = jnp.max(s, axis=1, keepdims=True)
            e = jnp.exp(s - m)
            l = jnp.sum(e, axis=1, keepdims=True)
            o_ref[rows, sl] = (_dot(e.astype(BF16), vband_ref[keys, sl]) * (1.0 / l)).astype(o_ref.dtype)
            lse_ref[rows, sl] = jnp.broadcast_to(m + jnp.log(l), (band, HEAD_DIM))
    kband_ref[0:band, :] = k_ref[(blocks - 1) * band:blocks * band, :]
    vband_ref[0:band, :] = v_ref[(blocks - 1) * band:blocks * band, :]


def dilated_group_attention(qk, v, window, dilation, n_heads, *, blocks=4):
    dil, L, W = qk.shape
    band = window // dilation
    width = n_heads * HEAD_DIM
    assert dil == dilation and W == 2 * width and v.shape == (dil, L, width)
    blocks = min(blocks, L // band)
    rows = blocks * band
    assert L % rows == 0

    def spec(t):
        return pl.BlockSpec((None, rows, width), lambda r, j: (r, j, t))

    return pl.pallas_call(
        functools.partial(_dilated_kernel, band=band, n_heads=n_heads, blocks=blocks),
        grid=(dil, L // rows),
        in_specs=[spec(0), spec(1), spec(0)],
        out_specs=[pl.BlockSpec((None, rows, width), lambda r, j: (r, j, 0))] * 2,
        out_shape=[jax.ShapeDtypeStruct((dil, L, width), BF16),
                   jax.ShapeDtypeStruct((dil, L, width), F32)],
        scratch_shapes=[pltpu.VMEM(((blocks + 1) * band, width), BF16)] * 2,
        compiler_params=_params("parallel", "arbitrary"),
        name="dilated_attention",
    )(qk, qk, v)


def _combine_kernel(*refs, dilations):
    n = len(dilations)
    o_refs, l_refs, out_ref = refs[:n], refs[n:2 * n], refs[2 * n]
    scratch = list(refs[2 * n + 1:])
    tm, width = out_ref.shape

    views = []
    for g, dil in enumerate(dilations):
        if dil == 1:
            views.append((lambda sl, o=o_refs[g]: o[0, :, sl].astype(F32), lambda sl, l=l_refs[g]: l[0, :, sl]))
            continue
        o_scr, l_scr = scratch.pop(0), scratch.pop(0)
        rows = tm // dil
        for c in range(width // LANES):
            sl = slice(c * LANES, (c + 1) * LANES)
            for r in range(dil):
                o_scr[c, pl.ds(r, rows, stride=dil), :] = o_refs[g][r, :, sl].astype(F32)
                l_scr[c, pl.ds(r, rows, stride=dil), :] = l_refs[g][r, :, sl]
        views.append((lambda sl, s=o_scr: s[sl.start // LANES], lambda sl, s=l_scr: s[sl.start // LANES]))

    for c in range(width // LANES):
        sl = slice(c * LANES, (c + 1) * LANES)
        lses = [lv(sl) for _, lv in views]
        top = functools.reduce(jnp.maximum, lses)
        ws = [jnp.exp(l - top) for l in lses]
        inv = 1.0 / functools.reduce(jnp.add, ws)
        acc = (ws[0] * inv) * views[0][0](sl)
        for w, (ov, _) in zip(ws[1:], views[1:]):
            acc = acc + (w * inv) * ov(sl)
        out_ref[:, sl] = acc.astype(out_ref.dtype)


def combine_groups(outs, lses, dilations, *, tm=256):
    width = outs[0].shape[2]
    S = outs[0].shape[0] * outs[0].shape[1]
    tm = min(tm, S)
    assert S % tm == 0 and all(tm % (8 * d) == 0 for d in dilations)
    specs = [pl.BlockSpec((d, tm // d, width), lambda i: (0, i, 0)) for d in dilations]
    scratch = []
    for d in dilations:
        if d > 1:
            scratch += [pltpu.VMEM((width // LANES, tm, LANES), F32)] * 2
    return pl.pallas_call(
        functools.partial(_combine_kernel, dilations=tuple(dilations)),
        grid=(S // tm,),
        in_specs=specs + specs,
        out_specs=pl.BlockSpec((tm, width), lambda i: (i, 0)),
        out_shape=jax.ShapeDtypeStruct((S, width), BF16),
        scratch_shapes=scratch,
        compiler_params=_params("parallel"),
        name="combine_groups",
    )(*outs, *lses)


def kernel(x, positions, norm_mix, norm_mlp, norm_final, mlp_w_up, mlp_w_down, moba_w_qkv, moba_w_o,
           conv_w_in, conv_w, conv_w_out, dil_w_qkv, dil_w_o):
    B, S, D = x.shape
    assert B == 1 and D % HEAD_DIM == 0
    n_heads = D // HEAD_DIM
    depth = norm_mix.shape[0]
    h = x[0]
    tables = rope_tables(positions[0])
    mixers = ((moba_w_qkv, moba_w_o), (conv_w_in, conv_w_out), (dil_w_qkv, dil_w_o))

    def f32_weights(i):
        kind, j = i % N_MIXERS, i // N_MIXERS
        return [(mixers[kind][0], j), (mixers[kind][1], j), (mlp_w_up, i), (mlp_w_down, i)]

    (w0, idx0), *late0 = f32_weights(0)
    weights = [w0[idx0:idx0 + 1].astype(BF16), None, None, None]
    for i in range(depth):
        kind = i % N_MIXERS
        w_in, w_out, w_up, w_down = weights
        if kind == 0:
            qk, v = norm_matmul_rope(h, norm_mix[i], w_in, 0, tables)
            halves, casts = moba_attention(qk[0], v[0], n_heads, late0 if i == 0 else ())
            if i == 0:
                w_out, w_up, w_down = casts
            h = matmul_residual(halves, w_out, 0, h)
        elif kind == 1:
            a = conv_mix(h, norm_mix[i], w_in, 0, conv_w[i // N_MIXERS])
            h = matmul_residual([a], w_out, 0, h)
        else:
            outs, lses = [], []
            for g, (window, dilation) in enumerate(DIL_GROUPS):
                qk, v = norm_matmul_rope(h, norm_mix[i], w_in, 0, tables, sec0=3 * g, dilation=dilation)
                o_g, lse_g = dilated_group_attention(qk, v, window, dilation, n_heads)
                outs.append(o_g)
                lses.append(lse_g)
            o = combine_groups(outs, lses, [d for _, d in DIL_GROUPS])
            h = matmul_residual([o], w_out, 0, h)
        last = i == depth - 1
        h, weights = mlp_residual(h, norm_mlp[i], w_up, w_down, 0, norm_final if last else None,
                                  () if last else f32_weights(i + 1))
    return h[None]
```
